```python
import jax, jax.numpy as jnp
from jax import lax
import numpy as np

D_MODEL = 4096
BATCH = 16
SEQ = 256
DEPTH = 4
DEC_BATCH = 8
DEC_SEQ = 1024
PAST_LEN = 256

GRID_W = 64
N_MIXERS = 3
HEAD_DIM = 128
N_Q_HEADS = 32
N_KV_HEADS = 8
N_GROUPS = N_Q_HEADS // N_KV_HEADS
WINDOW = 128
Q_BLOCK = 128
ROPE_THETA = 10000.0
MLA_HEADS = 64
MLA_Q_LORA = 1024
MLA_KV_LORA = 512
MLA_NOPE = 128
MLA_ROPE = 64
MLA_V = 128
D_FF = 11008
N_MOD = 9
N_A = (DEPTH + 2) // 3
N_B = (DEPTH + 1) // 3
N_C = DEPTH // 3
N_GQA = N_A + N_B
DEEPNORM_ALPHA = (2 * DEPTH) ** 0.25
DEEPNORM_BETA = (8 * DEPTH) ** -0.25
NEG_INF = -1e30

kernel_name = "hybrid_dit_prefix_step"


def rms_norm(x, g, eps=1e-6):
    xf = x.astype(jnp.float32)
    y = xf * lax.rsqrt(jnp.mean(xf * xf, axis=-1, keepdims=True) + eps)
    return (y * g.astype(jnp.float32)).astype(x.dtype)


def layer_norm(x, g, b, eps=1e-5):
    xf = x.astype(jnp.float32)
    xc = xf - jnp.mean(xf, axis=-1, keepdims=True)
    y = xc * lax.rsqrt(jnp.mean(xc * xc, axis=-1, keepdims=True) + eps)
    return (y * g.astype(jnp.float32) + b.astype(jnp.float32)).astype(x.dtype)


def axial_rope_tables(n_tok, rot_dim, dtype):
    n_rows = n_tok // GRID_W
    row = jnp.repeat(jnp.arange(n_rows, dtype=jnp.float32), GRID_W)
    col = jnp.tile(jnp.arange(GRID_W, dtype=jnp.float32), n_rows)
    quarter = rot_dim // 4
    inv_freq = ROPE_THETA ** (-jnp.arange(quarter, dtype=jnp.float32) / quarter)
    ang_r = row[:, None] * inv_freq
    ang_c = col[:, None] * inv_freq
    tabs = (jnp.cos(ang_r), jnp.sin(ang_r), jnp.cos(ang_c), jnp.sin(ang_c))
    return tuple(t[:, None, :].astype(dtype) for t in tabs)


def _rope_half(x, cos, sin):
    h = x.shape[-1] // 2
    x1, x2 = x[..., :h], x[..., h:]
    return jnp.concatenate([x1 * cos - x2 * sin, x2 * cos + x1 * sin], axis=-1)


def apply_axial_rope(x, tabs):
    cos_r, sin_r, cos_c, sin_c = tabs
    h = x.shape[-1] // 2
    return jnp.concatenate([_rope_half(x[..., :h], cos_r, sin_r),
                            _rope_half(x[..., h:], cos_c, sin_c)], axis=-1)


def dense_attend(q_sets, kv_sets, sink, scale):
    b, s, hk, g, _ = q_sets[0].shape
    n_blk = s // Q_BLOCK
    lens = [k.shape[1] for k, _ in kv_sets]

    def one_block(j):
        start = j * Q_BLOCK
        logits = [jnp.einsum("bqhgd,bkhd->bhgqk", lax.dynamic_slice_in_dim(q, start, Q_BLOCK, axis=1), k,
                             preferred_element_type=jnp.float32) * scale
                  for q, (k, _) in zip(q_sets, kv_sets)]
        if sink is not None:
            logits.append(jnp.broadcast_to(sink, (b, hk, g, Q_BLOCK, 1)))
        p = jax.nn.softmax(jnp.concatenate(logits, axis=-1), axis=-1)
        offs = np.cumsum([0] + lens)
        outs = [jnp.einsum("bhgqk,bkhd->bqhgd", p[..., int(offs[n]):int(offs[n + 1])].astype(v.dtype), v)
                for n, (_, v) in enumerate(kv_sets)]
        return sum(outs[1:], outs[0])

    blocks = lax.map(one_block, jnp.arange(n_blk))
    return jnp.moveaxis(blocks, 0, 1).reshape(b, s, hk, g, -1)


def banded_attend(q_rot, q_raw, k_rot, v, k_ctx, v_ctx, sink, scale):
    b, s, hk, g, _ = q_rot.shape
    n_blk = s // Q_BLOCK
    span = Q_BLOCK + 2 * WINDOW
    n_ctx = k_ctx.shape[1]
    pad = ((0, 0), (WINDOW, WINDOW), (0, 0), (0, 0))
    k_pad = jnp.pad(k_rot, pad)
    v_pad = jnp.pad(v, pad)

    def one_block(j):
        start = j * Q_BLOCK
        qb = lax.dynamic_slice_in_dim(q_rot, start, Q_BLOCK, axis=1)
        qc = lax.dynamic_slice_in_dim(q_raw, start, Q_BLOCK, axis=1)
        kb = lax.dynamic_slice_in_dim(k_pad, start, span, axis=1)
        vb = lax.dynamic_slice_in_dim(v_pad, start, span, axis=1)
        q_pos = start + jnp.arange(Q_BLOCK)
        k_pos = start - WINDOW + jnp.arange(span)
        valid = ((jnp.abs(q_pos[:, None] - k_pos[None, :]) <= WINDOW)
                 & (k_pos >= 0)[None, :] & (k_pos < s)[None, :])
        l_loc = jnp.einsum("bqhgd,bkhd->bhgqk", qb, kb, preferred_element_type=jnp.float32) * scale
        l_loc = jnp.where(valid, l_loc, NEG_INF)
        l_ctx = jnp.einsum("bqhgd,bkhd->bhgqk", qc, k_ctx, preferred_element_type=jnp.float32) * scale
        l_sink = jnp.broadcast_to(sink, (b, hk, g, Q_BLOCK, 1))
        p = jax.nn.softmax(jnp.concatenate([l_loc, l_ctx, l_sink], axis=-1), axis=-1)
        return (jnp.einsum("bhgqk,bkhd->bqhgd", p[..., :span].astype(v.dtype), vb)
                + jnp.einsum("bhgqk,bkhd->bqhgd", p[..., span:span + n_ctx].astype(v.dtype), v_ctx))

    blocks = lax.map(one_block, jnp.arange(n_blk))
    return jnp.moveaxis(blocks, 0, 1).reshape(b, s, hk, g, -1)


def _group(q):
    b, s, _, d = q.shape
    return q.reshape(b, s, N_KV_HEADS, N_GROUPS, d)


def _merge(o, w_o):
    b, s = o.shape[:2]
    return o.reshape(b, s, -1) @ w_o


def _gqa_qkv(h, w_qkv):
    b, s, _ = h.shape
    qkv = h @ w_qkv
    nq, nk = N_Q_HEADS * HEAD_DIM, N_KV_HEADS * HEAD_DIM
    q = qkv[..., :nq].reshape(b, s, N_Q_HEADS, HEAD_DIM)
    k = qkv[..., nq:nq + nk].reshape(b, s, N_KV_HEADS, HEAD_DIM)
    v = qkv[..., nq + nk:].reshape(b, s, N_KV_HEADS, HEAD_DIM)
    return q, k, v


def attn_a(h, w_qkv, w_o, q_norm, k_norm, ctx_kv=None, tabs=None):
    q, k, v = _gqa_qkv(h, w_qkv)
    q = rms_norm(q, q_norm)
    k = rms_norm(k, k_norm)
    scale = HEAD_DIM ** -0.5
    if ctx_kv is None:
        o = dense_attend([_group(q)], [(k, v)], None, scale)
        return _merge(o, w_o), (k, v)
    k_ctx, v_ctx = ctx_kv
    o = dense_attend([_group(apply_axial_rope(q, tabs)), _group(q)],
                     [(apply_axial_rope(k, tabs), v), (k_ctx, v_ctx)], None, scale)
    return _merge(o, w_o), None


def attn_b(h, w_qkv, w_o, sink, ctx_kv=None, tabs=None):
    q, k, v = _gqa_qkv(h, w_qkv)
    scale = HEAD_DIM ** -0.5
    sink_l = sink.astype(jnp.float32).reshape(1, N_KV_HEADS, N_GROUPS, 1, 1)
    if ctx_kv is None:
        o = dense_attend([_group(q)], [(k, v)], sink_l, scale)
        return _merge(o, w_o), (k, v)
    k_ctx, v_ctx = ctx_kv
    o = banded_attend(_group(apply_axial_rope(q, tabs)), _group(q), apply_axial_rope(k, tabs), v,
                      k_ctx, v_ctx, sink_l, scale)
    return _merge(o, w_o), None


def _mla_expand(c_kv, k_rope, w_kv_b):
    b, t, _ = c_kv.shape
    kv = (c_kv @ w_kv_b).reshape(b, t, MLA_HEADS, MLA_NOPE + MLA_V)
    k = jnp.concatenate([kv[..., :MLA_NOPE],
                         jnp.broadcast_to(k_rope[:, :, None, :], (b, t, MLA_HEADS, MLA_ROPE))], axis=-1)
    return k, kv[..., MLA_NOPE:]


def attn_c(h, w_q_a, q_a_norm, w_q_b, w_kv_a, kv_norm, w_kv_b, w_o, ctx_latent=None, tabs=None):
    b, s, _ = h.shape
    q = (rms_norm(h @ w_q_a, q_a_norm) @ w_q_b).reshape(b, s, MLA_HEADS, MLA_NOPE + MLA_ROPE)
    kv_a = h @ w_kv_a
    c_kv = rms_norm(kv_a[..., :MLA_KV_LORA], kv_norm)
    k_rope = kv_a[..., MLA_KV_LORA:]
    scale = (MLA_NOPE + MLA_ROPE) ** -0.5
    if ctx_latent is None:
        k, v = _mla_expand(c_kv, k_rope, w_kv_b)
        o = dense_attend([q[:, :, :, None]], [(k, v)], None, scale)
        return _merge(o, w_o), (c_kv, k_rope)
    ckv_ctx, krope_ctx = ctx_latent
    q_rot = jnp.concatenate([q[..., :MLA_NOPE], apply_axial_rope(q[..., MLA_NOPE:], tabs)], axis=-1)
    k_rope_rot = apply_axial_rope(k_rope[:, :, None, :], tabs)[:, :, 0]
    k_lat, v_lat = _mla_expand(c_kv, k_rope_rot, w_kv_b)
    k_ctx, v_ctx = _mla_expand(ckv_ctx, krope_ctx, w_kv_b)
    o = dense_attend([q_rot[:, :, :, None], q[:, :, :, None]],
                     [(k_lat, v_lat), (k_ctx, v_ctx)], None, scale)
    return _merge(o, w_o), None


def swiglu(x, w_gu, w_dn):
    gate, up = jnp.split(x @ w_gu, 2, axis=-1)
    return (jax.nn.silu(gate) * up) @ w_dn


def adaln(cvec, w, b):
    return (jax.nn.silu(cvec) @ w + b).reshape(cvec.shape[0], N_MOD, -1)


def trunk_layer(x, mod, ln_g, ln_b, w_gu, w_dn, mixer):
    m = [mod[:, i, None, :] for i in range(N_MOD)]
    x = layer_norm(DEEPNORM_ALPHA * x + 0.5 * m[2] * swiglu(x * (1 + m[1]) + m[0], w_gu[0], w_dn[0]),
                   ln_g[0], ln_b[0])
    y, ctx_out = mixer(x * (1 + m[4]) + m[3])
    x = layer_norm(DEEPNORM_ALPHA * x + m[5] * y, ln_g[1], ln_b[1])
    x = layer_norm(DEEPNORM_ALPHA * x + 0.5 * m[8] * swiglu(x * (1 + m[7]) + m[6], w_gu[1], w_dn[1]),
                   ln_g[2], ln_b[2])
    return x, ctx_out


def setup_inputs(seed: int = 0) -> dict:
    key = jax.random.key(seed)
    ks = iter(jax.random.split(key, 40))

    def nrm(shape, scale=1.0):
        return jax.random.normal(next(ks), shape, jnp.float32) * scale

    d = D_MODEL
    qkv_w = (N_Q_HEADS + 2 * N_KV_HEADS) * HEAD_DIM
    return {
        "x_prompt": nrm((BATCH, SEQ, d)),
        "x_sample": nrm((DEC_BATCH, DEC_SEQ, d)),
        "cache_k": nrm((DEC_BATCH, N_GQA, PAST_LEN, N_KV_HEADS, HEAD_DIM)),
        "cache_v": nrm((DEC_BATCH, N_GQA, PAST_LEN, N_KV_HEADS, HEAD_DIM)),
        "cache_ckv": nrm((DEC_BATCH, N_C, PAST_LEN, MLA_KV_LORA)),
        "cache_krope": nrm((DEC_BATCH, N_C, PAST_LEN, MLA_ROPE)),
        "c": nrm((DEC_BATCH, d)),
        "c_ctx": nrm((d,)),
        "w_ada": nrm((DEPTH, d, N_MOD * d), 0.5 * d ** -0.5),
        "b_ada": nrm((DEPTH, N_MOD * d), 0.02),
        "ln_g": 1.0 + nrm((DEPTH, 3, d), 0.02),
        "ln_b": nrm((DEPTH, 3, d), 0.02),
        "ffn_w_gu": nrm((DEPTH, 2, d, 2 * D_FF), d ** -0.5),
        "ffn_w_dn": nrm((DEPTH, 2, D_FF, d), DEEPNORM_BETA * D_FF ** -0.5),
        "a_w_qkv": nrm((N_A, d, qkv_w), d ** -0.5),
        "a_w_o": nrm((N_A, N_Q_HEADS * HEAD_DIM, d), DEEPNORM_BETA * (N_Q_HEADS * HEAD_DIM) ** -0.5),
        "a_q_norm": 1.0 + nrm((N_A, HEAD_DIM), 0.02),
        "a_k_norm": 1.0 + nrm((N_A, HEAD_DIM), 0.02),
        "b_w_qkv": nrm((N_B, d, qkv_w), d ** -0.5),
        "b_w_o": nrm((N_B, N_Q_HEADS * HEAD_DIM, d), DEEPNORM_BETA * (N_Q_HEADS * HEAD_DIM) ** -0.5),
        "b_sink": nrm((N_B, N_Q_HEADS), 0.5),
        "c_w_q_a": nrm((N_C, d, MLA_Q_LORA), d ** -0.5),
        "c_q_a_norm": 1.0 + nrm((N_C, MLA_Q_LORA), 0.02),
        "c_w_q_b": nrm((N_C, MLA_Q_LORA, MLA_HEADS * (MLA_NOPE + MLA_ROPE)), MLA_Q_LORA ** -0.5),
        "c_w_kv_a": nrm((N_C, d, MLA_KV_LORA + MLA_ROPE), d ** -0.5),
        "c_kv_norm": 1.0 + nrm((N_C, MLA_KV_LORA), 0.02),
        "c_w_kv_b": nrm((N_C, MLA_KV_LORA, MLA_HEADS * (MLA_NOPE + MLA_V)), MLA_KV_LORA ** -0.5),
        "c_w_o": nrm((N_C, MLA_HEADS * MLA_V, d), DEEPNORM_BETA * (MLA_HEADS * MLA_V) ** -0.5),
    }


def reference(x_prompt, x_sample, cache_k, cache_v, cache_ckv, cache_krope, c, c_ctx,
              w_ada, b_ada, ln_g, ln_b, ffn_w_gu, ffn_w_dn,
              a_w_qkv, a_w_o, a_q_norm, a_k_norm,
              b_w_qkv, b_w_o, b_sink,
              c_w_q_a, c_q_a_norm, c_w_q_b, c_w_kv_a, c_kv_norm, c_w_kv_b, c_w_o):
    n_lat = x_sample.shape[1]
    tabs_gqa = axial_rope_tables(n_lat, HEAD_DIM, x_sample.dtype)
    tabs_mla = axial_rope_tables(n_lat, MLA_ROPE, x_sample.dtype)
    xp, xs = x_prompt, x_sample
    new_k, new_v, new_ckv, new_krope = [], [], [], []
    for i in range(DEPTH):
        kind, idx = i % N_MIXERS, i // N_MIXERS
        mod_ctx = adaln(c_ctx[None], w_ada[i], b_ada[i])
        mod_lat = adaln(c, w_ada[i], b_ada[i])
        common = (ln_g[i], ln_b[i], ffn_w_gu[i], ffn_w_dn[i])
        if kind == 0:
            p = (a_w_qkv[idx], a_w_o[idx], a_q_norm[idx], a_k_norm[idx])
            slot = i - i // N_MIXERS
            xp, (k_c, v_c) = trunk_layer(xp, mod_ctx, *common, lambda h: attn_a(h, *p))
            xs, _ = trunk_layer(xs, mod_lat, *common,
                                lambda h: attn_a(h, *p, (cache_k[:, slot], cache_v[:, slot]), tabs_gqa))
            new_k.append(k_c)
            new_v.append(v_c)
        elif kind == 1:
            p = (b_w_qkv[idx], b_w_o[idx], b_sink[idx])
            slot = i - i // N_MIXERS
            xp, (k_c, v_c) = trunk_layer(xp, mod_ctx, *common, lambda h: attn_b(h, *p))
            xs, _ = trunk_layer(xs, mod_lat, *common,
                                lambda h: attn_b(h, *p, (cache_k[:, slot], cache_v[:, slot]), tabs_gqa))
            new_k.append(k_c)
            new_v.append(v_c)
        else:
            p = (c_w_q_a[idx], c_q_a_norm[idx], c_w_q_b[idx], c_w_kv_a[idx], c_kv_norm[idx],
                 c_w_kv_b[idx], c_w_o[idx])
            xp, (ckv_c, kr_c) = trunk_layer(xp, mod_ctx, *common, lambda h: attn_c(h, *p))
            xs, _ = trunk_layer(xs, mod_lat, *common,
                                lambda h: attn_c(h, *p, (cache_ckv[:, idx], cache_krope[:, idx]), tabs_mla))
            new_ckv.append(ckv_c)
            new_krope.append(kr_c)
    new_cache_k = jnp.stack(new_k, axis=1)
    new_cache_v = jnp.stack(new_v, axis=1)
    new_cache_ckv = jnp.stack(new_ckv, axis=1)
    new_cache_krope = jnp.stack(new_krope, axis=1)
    return (xp, xs, new_cache_k, new_cache_v, new_cache_ckv, new_cache_krope)
```

```python
import functools

import jax
import jax.numpy as jnp
from jax import lax
from jax.experimental import pallas as pl
from jax.experimental.pallas import tpu as pltpu

F32 = jnp.float32
BF16 = jnp.bfloat16

GRID_W = 64
N_MIXERS = 3
HEAD_DIM = 128
N_KV_HEADS = 8
WINDOW = 128
ROPE_THETA = 10000.0
MLA_NOPE = 128
MLA_ROPE = 64
MLA_V = 128
N_MOD = 9
NEG_INF = -1e30
RMS_EPS = 1e-6
LN_EPS = 1e-5

LANES = 128
SUBLANES = 8
VMEM_PHYSICAL_BYTES = 64 * 1024 * 1024
VMEM_CAP_BYTES = VMEM_PHYSICAL_BYTES - 6 * 1024 * 1024

TM_WIDE = 2048
TM = 1024
TN = 1024
TK = 1024
TN_FF = 512
TN_ADA = 512
TR_LN = 256
TQ_GQA = 256
TQ_MLA = 512


def _round_up(n, m):
    return -(-n // m) * m


def _tile(n, pref, unit=LANES):
    if n <= pref:
        return n
    t = (pref // unit) * unit
    while t > unit and n % t:
        t -= unit
    assert n % t == 0, (n, pref)
    return t


def _nbytes(shape, dtype):
    n = jnp.dtype(dtype).itemsize
    for s in shape:
        if s is not None:
            n *= s
    return n


def _cparams(semantics, block_bytes, temp_bytes=0):
    est = 2 * block_bytes + temp_bytes + 8 * 1024 * 1024
    return pltpu.CompilerParams(dimension_semantics=semantics,
                                vmem_limit_bytes=int(min(max(est, 16 * 1024 * 1024), VMEM_CAP_BYTES)))


def _dot(a, b):
    return jnp.dot(a, b, preferred_element_type=F32)


def _dot_nt(a, b):
    return lax.dot_general(a, b, (((1,), (1,)), ((), ())), preferred_element_type=F32)


def _sigmoid(x):
    return 1.0 / (1.0 + jnp.exp(-x))


class _Stream:
    def __init__(self, n_rows, seq, rows_per_mod, mod_row0):
        self.n_rows, self.seq, self.rows_per_mod, self.mod_row0 = n_rows, seq, rows_per_mod, mod_row0

    def mod_block(self, tm, width, col_block):
        rpm, row0 = self.rows_per_mod, self.mod_row0
        if tm > rpm:
            nb = tm // rpm
            assert tm % rpm == 0 and row0 % nb == 0
            return (nb, 1, width), lambda i, j: (row0 // nb + i, 0, col_block + j)
        assert rpm % tm == 0
        return (1, 1, width), lambda i, j: (row0 + (i * tm) // rpm, 0, col_block + j)


def _adaln_body(c_ref, w_ref, b_ref, o_ref):
    cv = c_ref[...]
    s = (cv * _sigmoid(cv)).astype(BF16)
    o_ref[...] = _dot(s, w_ref[...].astype(BF16)) + b_ref[...]


def _adaln(cvec, w_ada, b_ada):
    n_layers, d, n = w_ada.shape
    r = cvec.shape[0]
    tn = _tile(n, TN_ADA)
    blocks = _nbytes((r, d), F32) + _nbytes((d, tn), F32) + _nbytes((1, tn), F32) + _nbytes((r, tn), F32)
    return pl.pallas_call(
        _adaln_body,
        grid=(n_layers, n // tn),
        in_specs=[pl.BlockSpec((r, d), lambda l, j: (0, 0)),
                  pl.BlockSpec((None, d, tn), lambda l, j: (l, 0, j)),
                  pl.BlockSpec((None, 1, tn), lambda l, j: (l, 0, j))],
        out_specs=pl.BlockSpec((None, r, tn), lambda l, j: (l, 0, j)),
        out_shape=jax.ShapeDtypeStruct((n_layers, r, n), F32),
        compiler_params=_cparams(("parallel", "parallel"), blocks, _nbytes((d, tn), BF16)),
    )(cvec, w_ada, b_ada.reshape(n_layers, 1, n))


def _mm_body(*refs, n_w, n_extra, nk, epilogue):
    x_ref = refs[0]
    w_refs = refs[1:1 + n_w]
    e_refs = refs[1 + n_w:1 + n_w + n_extra]
    o_refs = refs[1 + n_w + n_extra:]

    def parts():
        x = x_ref[...]
        return [_dot(x, w[...]) for w in w_refs]

    if nk == 1:
        epilogue(parts, e_refs, o_refs)
        return

    k = pl.program_id(2)
    acc = o_refs[0]

    @pl.when(k == 0)
    def _():
        acc[...] = parts()[0]

    if nk > 2:
        @pl.when(jnp.logical_and(k > 0, k < nk - 1))
        def _():
            acc[...] += parts()[0]

    @pl.when(k == nk - 1)
    def _():
        epilogue(lambda: [acc[...] + parts()[0]], e_refs, o_refs)


def _matmul(x, ws, extras, outs, epilogue, *, tm, tn, tk, nj):
    m, kdim = x.shape
    assert m % tm == 0 and kdim % tk == 0, (x.shape, tm, tk)
    nk = kdim // tk
    if nk > 1:
        assert len(ws) == 1 and outs[0][0].dtype == F32

    def ij(im):
        return lambda i, j, k: im(i, j)

    in_specs = [pl.BlockSpec((tm, tk), lambda i, j, k: (i, k))]
    in_specs += [pl.BlockSpec((tk, tn), lambda i, j, k, off=off: (k, off + j)) for _, off in ws]
    in_specs += [pl.BlockSpec(blk, ij(im)) for _, blk, im in extras]
    out_specs = [pl.BlockSpec(blk, ij(im)) for _, blk, im in outs]
    blocks = _nbytes((tm, tk), x.dtype) + sum(_nbytes((tk, tn), w.dtype) for w, _ in ws)
    blocks += sum(_nbytes(blk, a.dtype) for a, blk, _ in extras) + sum(_nbytes(blk, s.dtype) for s, blk, _ in outs)
    res = pl.pallas_call(
        functools.partial(_mm_body, n_w=len(ws), n_extra=len(extras), nk=nk, epilogue=epilogue),
        grid=(m // tm, nj, nk),
        in_specs=in_specs,
        out_specs=out_specs,
        out_shape=[s for s, _, _ in outs],
        compiler_params=_cparams(("parallel", "parallel", "arbitrary"), blocks, (len(ws) + 1) * _nbytes((tm, tn), F32)),
    )(x, *[w for w, _ in ws], *[a for a, _, _ in extras])
    return res


def _ep_store(parts, e_refs, o_refs):
    o_refs[0][...] = parts()[0].astype(o_refs[0].dtype)


def _ep_swiglu(parts, e_refs, o_refs):
    g, u = parts()
    o_refs[0][...] = (g * _sigmoid(g) * u).astype(o_refs[0].dtype)


def _ep_residual(parts, e_refs, o_refs, *, alpha, gscale):
    x_ref, m_ref = e_refs
    y = parts()[0]
    nb = m_ref.shape[0]
    rows = y.shape[0] // nb
    for r in range(nb):
        sl = slice(r * rows, (r + 1) * rows)
        o_refs[0][sl, :] = alpha * x_ref[sl, :] + (gscale * m_ref[r]) * y[sl, :]


def _ep_headnorm(parts, e_refs, o_refs, *, n_norm_blocks):
    g_ref = e_refs[0]
    j = pl.program_id(1)

    @pl.when(j < n_norm_blocks)
    def _():
        y = parts()[0]
        for h in range(y.shape[1] // HEAD_DIM):
            sl = slice(h * HEAD_DIM, (h + 1) * HEAD_DIM)
            yh = y[:, sl]
            r = lax.rsqrt(jnp.mean(yh * yh, axis=-1, keepdims=True) + RMS_EPS)
            o_refs[0][:, sl] = yh * r * g_ref[0:1, sl]

    @pl.when(j >= n_norm_blocks)
    def _():
        o_refs[0][...] = parts()[0]


def _ep_rownorm(parts, e_refs, o_refs):
    y = parts()[0]
    r = lax.rsqrt(jnp.mean(y * y, axis=-1, keepdims=True) + RMS_EPS)
    o_refs[0][...] = (y * r * e_refs[0][0:1, :]).astype(o_refs[0].dtype)


def _ep_kva(parts, e_refs, o_refs, *, lora):
    y = parts()[0]
    c = y[:, :lora]
    r = lax.rsqrt(jnp.mean(c * c, axis=-1, keepdims=True) + RMS_EPS)
    o_refs[0][...] = c * r * e_refs[0][0:1, :]
    o_refs[1][...] = y[:, lora:]


def _mm_plain(x, w, out_dtype, *, tm=TM, tn=TN):
    m, kdim = x.shape
    n = w.shape[1]
    tm, tn = _tile(m, tm, SUBLANES), _tile(n, tn)
    out = (jax.ShapeDtypeStruct((m, n), out_dtype), (tm, tn), lambda i, j: (i, j))
    return _matmul(x, [(w, 0)], [], [out], _ep_store, tm=tm, tn=tn, tk=kdim, nj=n // tn)[0]


def _mm_swiglu(h, w_gu, f_pad):
    m, d = h.shape
    tm, tn = _tile(m, TM, SUBLANES), _tile(f_pad, TN_FF)
    nj = f_pad // tn
    out = (jax.ShapeDtypeStruct((m, f_pad), BF16), (tm, tn), lambda i, j: (i, j))
    return _matmul(h, [(w_gu, 0), (w_gu, nj)], [], [out], _ep_swiglu, tm=tm, tn=tn, tk=d, nj=nj)[0]


def _mm_residual(a, w, x, mod, stream, chunk, gscale, alpha):
    m, kdim = a.shape
    d = w.shape[1]
    tm, tn, tk = _tile(m, TM_WIDE, SUBLANES), _tile(d, TN), _tile(kdim, TK)
    if tm > stream.rows_per_mod and tm % stream.rows_per_mod:
        tm = stream.rows_per_mod
    mblk, mim = stream.mod_block(tm, tn, chunk * (d // tn))
    extras = [(x, (tm, tn), lambda i, j: (i, j)), (mod, mblk, mim)]
    out = (jax.ShapeDtypeStruct((m, d), F32), (tm, tn), lambda i, j: (i, j))
    ep = functools.partial(_ep_residual, alpha=alpha, gscale=gscale)
    return _matmul(a, [(w, 0)], extras, [out], ep, tm=tm, tn=tn, tk=tk, nj=d // tn)[0]


def _ln_body(*refs, with_h):
    if with_h:
        z_ref, g_ref, b_ref, sh_ref, sc_ref, x_ref, h_ref = refs
    else:
        z_ref, g_ref, b_ref, x_ref = refs
    z = z_ref[...]
    zc = z - jnp.mean(z, axis=-1, keepdims=True)
    y = zc * lax.rsqrt(jnp.mean(zc * zc, axis=-1, keepdims=True) + LN_EPS)
    y = y * g_ref[...] + b_ref[...]
    x_ref[...] = y
    if with_h:
        h_ref[...] = (y * (1 + sc_ref[0]) + sh_ref[0]).astype(BF16)


def _ln_mod(z, g, b, mod=None, stream=None, shift_chunk=None):
    m, d = z.shape
    tr = _tile(m, TR_LN, SUBLANES)
    with_h = mod is not None
    row = pl.BlockSpec((tr, d), lambda i: (i, 0))
    vec = pl.BlockSpec((1, d), lambda i: (0, 0))
    in_specs, args = [row, vec, vec], [z, g.reshape(1, d), b.reshape(1, d)]
    out_specs, out_shape = [row], [jax.ShapeDtypeStruct((m, d), F32)]
    blocks = 2 * _nbytes((tr, d), F32)
    if with_h:
        for chunk in (shift_chunk, shift_chunk + 1):
            blk, im = stream.mod_block(tr, d, chunk)
            in_specs.append(pl.BlockSpec(blk, lambda i, im=im: im(i, 0)))
            args.append(mod)
        out_specs.append(row)
        out_shape.append(jax.ShapeDtypeStruct((m, d), BF16))
        blocks += _nbytes((tr, d), BF16)
    res = pl.pallas_call(
        functools.partial(_ln_body, with_h=with_h),
        grid=(m // tr,), in_specs=in_specs, out_specs=out_specs, out_shape=out_shape,
        compiler_params=_cparams(("parallel",), blocks, 2 * _nbytes((tr, d), F32)),
    )(*args)
    return (res[0], res[1]) if with_h else (res[0], None)


def _modulate_body(x_ref, sh_ref, sc_ref, h_ref):
    h_ref[...] = (x_ref[...] * (1 + sc_ref[0]) + sh_ref[0]).astype(BF16)


def _modulate(x, mod, stream, shift_chunk):
    m, d = x.shape
    tr = _tile(m, TR_LN, SUBLANES)
    row = pl.BlockSpec((tr, d), lambda i: (i, 0))
    in_specs, args = [row], [x]
    for chunk in (shift_chunk, shift_chunk + 1):
        blk, im = stream.mod_block(tr, d, chunk)
        in_specs.append(pl.BlockSpec(blk, lambda i, im=im: im(i, 0)))
        args.append(mod)
    return pl.pallas_call(
        _modulate_body, grid=(m // tr,), in_specs=in_specs, out_specs=row,
        out_shape=jax.ShapeDtypeStruct((m, d), BF16),
        compiler_params=_cparams(("parallel",), _nbytes((tr, d), F32) + _nbytes((tr, d), BF16)),
    )(*args)


def _rope_tables(n_tok, rot_dim):
    n_rows = n_tok // GRID_W
    row = jnp.repeat(jnp.arange(n_rows, dtype=F32), GRID_W)
    col = jnp.tile(jnp.arange(GRID_W, dtype=F32), n_rows)
    quarter = rot_dim // 4
    inv_freq = ROPE_THETA ** (-jnp.arange(quarter, dtype=F32) / quarter)
    ang_r, ang_c = row[:, None] * inv_freq, col[:, None] * inv_freq
    cos = jnp.concatenate([jnp.cos(ang_r)] * 2 + [jnp.cos(ang_c)] * 2, axis=-1)
    sin = jnp.concatenate([-jnp.sin(ang_r), jnp.sin(ang_r), -jnp.sin(ang_c), jnp.sin(ang_c)], axis=-1)
    reps = LANES // rot_dim
    return jnp.tile(cos, (1, reps)), jnp.tile(sin, (1, reps))


def _rope(x, cos, sin, quarter):
    lane = lax.broadcasted_iota(jnp.int32, x.shape, x.ndim - 1)
    first = (lane & quarter) == 0
    partner = jnp.where(first, pltpu.roll(x, LANES - quarter, axis=x.ndim - 1), pltpu.roll(x, quarter, axis=x.ndim - 1))
    return x * cos + partner * sin


def _softmax_pv(s_loc, v_loc, s_ctx, v_ctx, sink):
    g, tq, _ = s_loc.shape
    m = jnp.max(s_loc, axis=-1, keepdims=True)
    if s_ctx is not None:
        m = jnp.maximum(m, jnp.max(s_ctx, axis=-1, keepdims=True))
    if sink is not None:
        m = jnp.maximum(m, sink)
    p = jnp.exp(s_loc - m)
    l = jnp.sum(p, axis=-1, keepdims=True)
    o = _dot(p.astype(BF16).reshape(g * tq, -1), v_loc)
    if s_ctx is not None:
        p = jnp.exp(s_ctx - m)
        l = l + jnp.sum(p, axis=-1, keepdims=True)
        o = o + _dot(p.astype(BF16).reshape(g * tq, -1), v_ctx)
    if sink is not None:
        l = l + jnp.exp(sink - m)
    return o.reshape(g, tq, -1) / l


def _gqa_body(*refs, seq, n_ctx, groups, tq, rope, banded, has_sink, scale):
    it = iter(refs)
    q_ref, k_ref, v_ref = next(it), next(it), next(it)
    kc_ref, vc_ref = (next(it), next(it)) if n_ctx else (None, None)
    cos_ref, sin_ref = (next(it), next(it)) if rope else (None, None)
    sink_ref = next(it) if has_sink else None
    o_ref = next(it)
    quarter = HEAD_DIM // 4

    k = k_ref[...]
    if rope:
        k = _rope(k, cos_ref[...], sin_ref[...], quarter)
    kb, vb = k.astype(BF16), v_ref[...].astype(BF16)
    kcb, vcb = (kc_ref[...].astype(BF16), vc_ref[...].astype(BF16)) if n_ctx else (None, None)
    sink = sink_ref[:, 0:1, 0:1] if has_sink else None

    for qb in range(seq // tq):
        r0 = qb * tq
        q_raw, q_rot = [], []
        for g in range(groups):
            q = q_ref[r0:r0 + tq, g * HEAD_DIM:(g + 1) * HEAD_DIM] * scale
            q_raw.append(q.astype(BF16))
            if rope:
                q_rot.append(_rope(q, cos_ref[r0:r0 + tq, :], sin_ref[r0:r0 + tq, :], quarter).astype(BF16))
        q_raw = jnp.concatenate(q_raw, axis=0)
        q_loc = jnp.concatenate(q_rot, axis=0) if rope else q_raw
        k0, k1 = (max(0, r0 - WINDOW), min(seq, r0 + tq + WINDOW)) if banded else (0, seq)
        s_loc = _dot_nt(q_loc, kb[k0:k1]).reshape(groups, tq, k1 - k0)
        if banded:
            q_pos = r0 + lax.broadcasted_iota(jnp.int32, (tq, k1 - k0), 0)
            k_pos = k0 + lax.broadcasted_iota(jnp.int32, (tq, k1 - k0), 1)
            s_loc = jnp.where((jnp.abs(q_pos - k_pos) <= WINDOW)[None], s_loc, NEG_INF)
        s_ctx = _dot_nt(q_raw, kcb).reshape(groups, tq, n_ctx) if n_ctx else None
        o = _softmax_pv(s_loc, vb[k0:k1], s_ctx, vcb, sink)
        for g in range(groups):
            o_ref[r0:r0 + tq, g * HEAD_DIM:(g + 1) * HEAD_DIM] = o[g].astype(o_ref.dtype)


def _gqa_attention(qkv, n_batch, seq, n_q_heads, *, ctx_kv=None, tabs=None, sink=None, banded=False):
    groups = n_q_heads // N_KV_HEADS
    tq = _tile(seq, TQ_GQA, SUBLANES)
    qw = groups * HEAD_DIM
    in_specs = [pl.BlockSpec((seq, qw), lambda b, j: (b, j)),
                pl.BlockSpec((seq, HEAD_DIM), lambda b, j: (b, n_q_heads + j)),
                pl.BlockSpec((seq, HEAD_DIM), lambda b, j: (b, n_q_heads + N_KV_HEADS + j))]
    args = [qkv, qkv, qkv]
    blocks = _nbytes((seq, qw + 2 * HEAD_DIM), F32) + _nbytes((seq, qw), BF16)
    n_ctx = 0
    if ctx_kv is not None:
        n_ctx = ctx_kv[0].shape[1]
        in_specs += [pl.BlockSpec((None, n_ctx, HEAD_DIM), lambda b, j: (b, 0, j))] * 2
        args += list(ctx_kv)
        blocks += 2 * _nbytes((n_ctx, HEAD_DIM), F32)
    if tabs is not None:
        in_specs += [pl.BlockSpec((seq, LANES), lambda b, j: (0, 0))] * 2
        args += list(tabs)
        blocks += 2 * _nbytes((seq, LANES), F32)
    if sink is not None:
        sink_b = jnp.broadcast_to(sink.astype(F32).reshape(N_KV_HEADS, groups, 1, 1), (N_KV_HEADS, groups, SUBLANES, LANES))
        in_specs.append(pl.BlockSpec((None, groups, SUBLANES, LANES), lambda b, j: (j, 0, 0, 0)))
        args.append(sink_b)
        blocks += _nbytes((groups, SUBLANES, LANES), F32)
    body = functools.partial(_gqa_body, seq=seq, n_ctx=n_ctx, groups=groups, tq=tq, rope=tabs is not None,
                             banded=banded, has_sink=sink is not None, scale=HEAD_DIM ** -0.5)
    temps = 3 * _nbytes((groups * tq, seq + n_ctx), F32)
    return pl.pallas_call(
        body, grid=(n_batch, N_KV_HEADS), in_specs=in_specs,
        out_specs=pl.BlockSpec((seq, qw), lambda b, j: (b, j)),
        out_shape=jax.ShapeDtypeStruct((n_batch * seq, n_q_heads * HEAD_DIM), BF16),
        compiler_params=_cparams(("parallel", "parallel"), blocks, temps),
    )(*args)


def _mla_body(*refs, seq, n_ctx, tq, rope, scale):
    it = iter(refs)
    qn_ref, qr_ref, c_ref, kr_ref, w_ref = next(it), next(it), next(it), next(it), next(it)
    cc_ref, krc_ref = (next(it), next(it)) if n_ctx else (None, None)
    cos_ref, sin_ref = (next(it), next(it)) if rope else (None, None)
    o_ref = next(it)
    quarter = MLA_ROPE // 4
    hw = MLA_NOPE + MLA_V

    def rope_keys(kr):
        return kr.astype(BF16), pltpu.roll(kr, MLA_ROPE, axis=1).astype(BF16)

    w = w_ref[...]
    kv = _dot(c_ref[...].astype(BF16), w)
    kr = kr_ref[...]
    if rope:
        kr = _rope(kr, cos_ref[...], sin_ref[...], quarter)
    kr_pair = rope_keys(kr)
    if n_ctx:
        kvc = _dot(cc_ref[...].astype(BF16), w)
        krc_pair = rope_keys(krc_ref[...])

    for qb in range(seq // tq):
        rows = slice(qb * tq, (qb + 1) * tq)
        qr = qr_ref[rows, :] * scale
        qr_raw = qr.astype(BF16)
        qr_loc = _rope(qr, cos_ref[rows, :], sin_ref[rows, :], quarter).astype(BF16) if rope else qr_raw
        for h in range(2):
            qn = (qn_ref[rows, h * MLA_NOPE:(h + 1) * MLA_NOPE] * scale).astype(BF16)
            kn = kv[:, h * hw:h * hw + MLA_NOPE].astype(BF16)
            vv = kv[:, h * hw + MLA_NOPE:(h + 1) * hw].astype(BF16)
            s_loc = (_dot_nt(qn, kn) + _dot_nt(qr_loc, kr_pair[h]))[None]
            s_ctx = vc = None
            if n_ctx:
                knc = kvc[:, h * hw:h * hw + MLA_NOPE].astype(BF16)
                vc = kvc[:, h * hw + MLA_NOPE:(h + 1) * hw].astype(BF16)
                s_ctx = (_dot_nt(qn, knc) + _dot_nt(qr_raw, krc_pair[h]))[None]
            o = _softmax_pv(s_loc, vv, s_ctx, vc, None)
            o_ref[rows, h * MLA_V:(h + 1) * MLA_V] = o[0].astype(o_ref.dtype)


def _mla_attention(q, ckv, krope, w_kv_b, n_batch, seq, n_heads, *, ctx=None, tabs=None):
    lora = ckv.shape[1]
    tq = _tile(seq, TQ_MLA, SUBLANES)
    n_pairs = n_heads // 2
    in_specs = [pl.BlockSpec((seq, 2 * MLA_NOPE), lambda b, p: (b, p)),
                pl.BlockSpec((seq, LANES), lambda b, p: (b, n_heads * MLA_NOPE // LANES + p)),
                pl.BlockSpec((seq, lora), lambda b, p: (b, 0)),
                pl.BlockSpec((seq, LANES), lambda b, p: (b, 0)),
                pl.BlockSpec((lora, 2 * (MLA_NOPE + MLA_V)), lambda b, p: (0, p))]
    args = [q, q, ckv, krope, w_kv_b]
    blocks = _nbytes((seq, 2 * MLA_NOPE + 2 * LANES + lora), F32) + _nbytes((lora, 2 * (MLA_NOPE + MLA_V)), BF16)
    blocks += _nbytes((seq, 2 * MLA_V), BF16)
    n_ctx = 0
    if ctx is not None:
        n_ctx = ctx[0].shape[1]
        in_specs += [pl.BlockSpec((None, n_ctx, lora), lambda b, p: (b, 0, 0)),
                     pl.BlockSpec((None, n_ctx, LANES), lambda b, p: (b, 0, 0))]
        args += list(ctx)
        blocks += _nbytes((n_ctx, lora + LANES), F32)
    if tabs is not None:
        in_specs += [pl.BlockSpec((seq, LANES), lambda b, p: (0, 0))] * 2
        args += list(tabs)
        blocks += 2 * _nbytes((seq, LANES), F32)
    body = functools.partial(_mla_body, seq=seq, n_ctx=n_ctx, tq=tq, rope=tabs is not None,
                             scale=(MLA_NOPE + MLA_ROPE) ** -0.5)
    temps = 3 * _nbytes((tq, seq + n_ctx), F32) + 2 * _nbytes((seq + n_ctx, 2 * (MLA_NOPE + MLA_V)), F32)
    return pl.pallas_call(
        body, grid=(n_batch, n_pairs), in_specs=in_specs,
        out_specs=pl.BlockSpec((seq, 2 * MLA_V), lambda b, p: (b, p)),
        out_shape=jax.ShapeDtypeStruct((n_batch * seq, n_heads * MLA_V), BF16),
        compiler_params=_cparams(("parallel", "parallel"), blocks, temps),
    )(*args)


def _mixer_gqa(h, x, mod, stream, n_batch, alpha, w_qkv, w_o, gain, *, ctx_kv, tabs, sink, banded):
    n_q_heads = w_o.shape[0] // HEAD_DIM
    kvw = N_KV_HEADS * HEAD_DIM
    m, d = h.shape
    n = w_qkv.shape[1]
    tm, tn = _tile(m, TM, SUBLANES), kvw
    out = (jax.ShapeDtypeStruct((m, n), F32), (tm, tn), lambda i, j: (i, j))
    if gain is None:
        qkv = _matmul(h, [(w_qkv, 0)], [], [out], _ep_store, tm=tm, tn=tn, tk=d, nj=n // tn)[0]
    else:
        ep = functools.partial(_ep_headnorm, n_norm_blocks=(n - kvw) // tn)
        extras = [(gain, (SUBLANES, tn), lambda i, j: (0, j))]
        qkv = _matmul(h, [(w_qkv, 0)], extras, [out], ep, tm=tm, tn=tn, tk=d, nj=n // tn)[0]
    o = _gqa_attention(qkv, n_batch, stream.seq, n_q_heads, ctx_kv=ctx_kv, tabs=tabs, sink=sink, banded=banded)
    z = _mm_residual(o, w_o, x, mod, stream, 5, 1.0, alpha)
    return z, qkv


def _mixer_mla(h, x, mod, stream, n_batch, alpha, p, *, ctx, tabs):
    n_heads = p["w_o"].shape[0] // MLA_V
    m, d = h.shape
    lora = p["kv_gain"].shape[1]
    q_lora = p["w_q_a"].shape[1]
    tm = _tile(m, TM, SUBLANES)
    out = (jax.ShapeDtypeStruct((m, q_lora), BF16), (tm, q_lora), lambda i, j: (i, 0))
    qa = _matmul(h, [(p["w_q_a"], 0)], [(p["q_gain"], (SUBLANES, q_lora), lambda i, j: (0, 0))], [out], _ep_rownorm,
                 tm=tm, tn=q_lora, tk=d, nj=1)[0]
    q = _mm_plain(qa, p["w_q_b"], F32, tm=TM, tn=2 * TN)
    kvw = p["w_kv_a"].shape[1]
    outs = [(jax.ShapeDtypeStruct((m, lora), F32), (tm, lora), lambda i, j: (i, 0)),
            (jax.ShapeDtypeStruct((m, LANES), F32), (tm, LANES), lambda i, j: (i, 0))]
    ckv, krope = _matmul(h, [(p["w_kv_a"], 0)], [(p["kv_gain"], (SUBLANES, lora), lambda i, j: (0, 0))], outs,
                         functools.partial(_ep_kva, lora=lora), tm=tm, tn=kvw, tk=d, nj=1)
    o = _mla_attention(q, ckv, krope, p["w_kv_b"], n_batch, stream.seq, n_heads, ctx=ctx, tabs=tabs)
    z = _mm_residual(o, p["w_o"], x, mod, stream, 5, 1.0, alpha)
    return z, ckv, krope


def _pad_cols(w, n):
    return jnp.pad(w, [(0, 0)] * (w.ndim - 1) + [(0, n - w.shape[-1])])


def kernel(x_prompt, x_sample, cache_k, cache_v, cache_ckv, cache_krope, c, c_ctx, w_ada, b_ada, ln_g, ln_b, ffn_w_gu, ffn_w_dn, a_w_qkv, a_w_o, a_q_norm, a_k_norm, b_w_qkv, b_w_o, b_sink, c_w_q_a, c_q_a_norm, c_w_q_b, c_w_kv_a, c_kv_norm, c_w_kv_b, c_w_o):
    n_ctx_b, seq_ctx, d = x_prompt.shape
    n_lat_b, seq_lat, _ = x_sample.shape
    depth = w_ada.shape[0]
    d_ff = ffn_w_dn.shape[2]
    f_pad = _round_up(d_ff, TK)
    alpha = (2 * depth) ** 0.25
    n_q_heads = a_w_o.shape[1] // HEAD_DIM
    kvw = N_KV_HEADS * HEAD_DIM
    past = cache_k.shape[2]

    streams = (_Stream(n_ctx_b * seq_ctx, seq_ctx, n_ctx_b * seq_ctx, n_lat_b), _Stream(n_lat_b * seq_lat, seq_lat, seq_lat, 0))
    n_batches = (n_ctx_b, n_lat_b)
    n_mod_rows = _round_up(n_lat_b + 1, 2 * SUBLANES)
    cvec = jnp.zeros((n_mod_rows, d), F32).at[:n_lat_b].set(c).at[n_lat_b].set(c_ctx)
    mod_all = _adaln(cvec, w_ada, b_ada).reshape(depth, n_mod_rows, 1, N_MOD * d)

    w_gu = _pad_cols(ffn_w_gu.astype(BF16).reshape(depth, 2, d, 2, d_ff), f_pad).reshape(depth, 2, d, 2 * f_pad)
    w_dn = jnp.pad(ffn_w_dn.astype(BF16), ((0, 0), (0, 0), (0, f_pad - d_ff), (0, 0)))

    tabs_gqa = _rope_tables(seq_lat, HEAD_DIM)
    tabs_mla = _rope_tables(seq_lat, MLA_ROPE)

    xs = [x_prompt.reshape(-1, d), x_sample.reshape(-1, d)]
    hs = [_modulate(xs[s], mod_all[0], streams[s], 0) for s in range(2)]
    new_k, new_v, new_ckv, new_krope = [], [], [], []

    for i in range(depth):
        kind, idx = i % N_MIXERS, i // N_MIXERS
        mod = mod_all[i]

        def ffn(s, which, next_mod, next_chunk):
            act = _mm_swiglu(hs[s], w_gu[i, which], f_pad)
            z = _mm_residual(act, w_dn[i, which], xs[s], mod, streams[s], 2 + 6 * which, 0.5, alpha)
            xs[s], hs[s] = _ln_mod(z, ln_g[i, 2 * which], ln_b[i, 2 * which], next_mod, streams[s], next_chunk)

        for s in range(2):
            ffn(s, 0, mod, 3)

        if kind in (0, 1):
            slot = i - i // N_MIXERS
            if kind == 0:
                w_qkv, w_o = a_w_qkv[idx].astype(BF16), a_w_o[idx].astype(BF16)
                gain = jnp.concatenate([jnp.tile(a_q_norm[idx], n_q_heads), jnp.tile(a_k_norm[idx], N_KV_HEADS),
                                        jnp.ones((kvw,), F32)])
                gain = jnp.broadcast_to(gain[None], (SUBLANES, gain.shape[0]))
                sink = None
            else:
                w_qkv, w_o = b_w_qkv[idx].astype(BF16), b_w_o[idx].astype(BF16)
                gain, sink = None, b_sink[idx]
            ctx_kv = (cache_k[:, slot].reshape(n_lat_b, past, kvw), cache_v[:, slot].reshape(n_lat_b, past, kvw))
            for s in range(2):
                lat = s == 1
                z, qkv = _mixer_gqa(hs[s], xs[s], mod, streams[s], n_batches[s], alpha, w_qkv, w_o, gain,
                                    ctx_kv=ctx_kv if lat else None, tabs=tabs_gqa if lat else None,
                                    sink=sink, banded=lat and kind == 1)
                xs[s], hs[s] = _ln_mod(z, ln_g[i, 1], ln_b[i, 1], mod, streams[s], 6)
                if not lat:
                    qw = n_q_heads * HEAD_DIM
                    new_k.append(qkv[:, qw:qw + kvw].reshape(n_ctx_b, seq_ctx, N_KV_HEADS, HEAD_DIM))
                    new_v.append(qkv[:, qw + kvw:].reshape(n_ctx_b, seq_ctx, N_KV_HEADS, HEAD_DIM))
        else:
            n_heads = c_w_o.shape[1] // MLA_V
            lora = c_kv_norm.shape[1]
            w_q_b = c_w_q_b[idx].astype(BF16).reshape(-1, n_heads, MLA_NOPE + MLA_ROPE)
            p = {
                "w_q_a": c_w_q_a[idx].astype(BF16),
                "q_gain": jnp.broadcast_to(c_q_a_norm[idx][None], (SUBLANES, c_q_a_norm.shape[1])),
                "w_q_b": jnp.concatenate([w_q_b[:, :, :MLA_NOPE].reshape(-1, n_heads * MLA_NOPE),
                                          w_q_b[:, :, MLA_NOPE:].reshape(-1, n_heads * MLA_ROPE)], axis=1),
                "w_kv_a": _pad_cols(c_w_kv_a[idx].astype(BF16), lora + LANES),
                "kv_gain": jnp.broadcast_to(c_kv_norm[idx][None], (SUBLANES, lora)),
                "w_kv_b": c_w_kv_b[idx].astype(BF16),
                "w_o": c_w_o[idx].astype(BF16),
            }
            ctx = (cache_ckv[:, idx], _pad_cols(cache_krope[:, idx], LANES))
            for s in range(2):
                lat = s == 1
                z, ckv, krope = _mixer_mla(hs[s], xs[s], mod, streams[s], n_batches[s], alpha, p,
                                           ctx=ctx if lat else None, tabs=tabs_mla if lat else None)
                xs[s], hs[s] = _ln_mod(z, ln_g[i, 1], ln_b[i, 1], mod, streams[s], 6)
                if not lat:
                    new_ckv.append(ckv.reshape(n_ctx_b, seq_ctx, lora))
                    new_krope.append(krope[:, :MLA_ROPE].reshape(n_ctx_b, seq_ctx, MLA_ROPE))

        last = i == depth - 1
        for s in range(2):
            ffn(s, 1, None if last else mod_all[i + 1], None if last else 0)

    return (xs[0].reshape(x_prompt.shape), xs[1].reshape(x_sample.shape),
            jnp.stack(new_k, axis=1), jnp.stack(new_v, axis=1), jnp.stack(new_ckv, axis=1), jnp.stack(new_krope, axis=1))
```

```python
import functools

import jax
import jax.numpy as jnp
from jax import lax
from jax.experimental import pallas as pl
from jax.experimental.pallas import tpu as pltpu

F32 = jnp.float32
BF16 = jnp.bfloat16

GRID_W = 64
N_MIXERS = 3
HEAD_DIM = 128
N_KV_HEADS = 8
WINDOW = 128
ROPE_THETA = 10000.0
MLA_NOPE = 128
MLA_ROPE = 64
MLA_V = 128
N_MOD = 9
NEG_INF = -1e30
RMS_EPS = 1e-6
LN_EPS = 1e-5

LANES = 128
SUBLANES = 8
VMEM_PHYSICAL_BYTES = 64 * 1024 * 1024
VMEM_CAP_BYTES = VMEM_PHYSICAL_BYTES - 6 * 1024 * 1024

TM_WIDE = 2048
TM = 1024
TN = 1024
TK = 1024
TN_FF = 256
TN_ADA = 512
TR_LN = 256
TQ_GQA = 256
TQ_MLA = 512
ATTN_ROWS_PER_STEP = 1024

LOG2E = 1.4426950408889634


def _round_up(n, m):
    return -(-n // m) * m


def _tile(n, pref, unit=LANES):
    if n <= pref:
        return n
    t = (pref // unit) * unit
    while t > unit and n % t:
        t -= unit
    assert n % t == 0, (n, pref)
    return t


def _nbytes(shape, dtype):
    n = jnp.dtype(dtype).itemsize
    for s in shape:
        if s is not None:
            n *= s
    return n


def _cparams(semantics, block_bytes, temp_bytes=0):
    est = 2 * block_bytes + temp_bytes + 8 * 1024 * 1024
    return pltpu.CompilerParams(dimension_semantics=semantics,
                                vmem_limit_bytes=int(min(max(est, 16 * 1024 * 1024), VMEM_CAP_BYTES)))


def _dot(a, b):
    return jnp.dot(a, b, preferred_element_type=F32)


def _dot_nt(a, b):
    return lax.dot_general(a, b, (((1,), (1,)), ((), ())), preferred_element_type=F32)


def _sigmoid(x):
    return 1.0 / (1.0 + jnp.exp(-x))


class _Stream:
    def __init__(self, n_rows, seq, rows_per_mod, mod_row0):
        self.n_rows, self.seq, self.rows_per_mod, self.mod_row0 = n_rows, seq, rows_per_mod, mod_row0

    def mod_block(self, tm, width, col_block):
        rpm, row0 = self.rows_per_mod, self.mod_row0
        if tm > rpm:
            nb = tm // rpm
            assert tm % rpm == 0 and row0 % nb == 0
            return (nb, 1, width), lambda i, j: (row0 // nb + i, 0, col_block + j)
        assert rpm % tm == 0
        return (1, 1, width), lambda i, j: (row0 + (i * tm) // rpm, 0, col_block + j)


def _adaln_body(c_ref, w_ref, b_ref, o_ref):
    cv = c_ref[...]
    s = (cv * _sigmoid(cv)).astype(BF16)
    o_ref[...] = _dot(s, w_ref[...].astype(BF16)) + b_ref[...]


def _adaln(cvec, w_ada, b_ada):
    n_layers, d, n = w_ada.shape
    r = cvec.shape[0]
    tn = _tile(n, TN_ADA)
    blocks = _nbytes((r, d), F32) + _nbytes((d, tn), F32) + _nbytes((1, tn), F32) + _nbytes((r, tn), F32)
    return pl.pallas_call(
        _adaln_body,
        grid=(n_layers, n // tn),
        in_specs=[pl.BlockSpec((r, d), lambda l, j: (0, 0)),
                  pl.BlockSpec((None, d, tn), lambda l, j: (l, 0, j)),
                  pl.BlockSpec((None, 1, tn), lambda l, j: (l, 0, j))],
        out_specs=pl.BlockSpec((None, r, tn), lambda l, j: (l, 0, j)),
        out_shape=jax.ShapeDtypeStruct((n_layers, r, n), F32),
        compiler_params=_cparams(("parallel", "parallel"), blocks, _nbytes((d, tn), BF16)),
    )(cvec, w_ada, b_ada.reshape(n_layers, 1, n))


def _mm_body(*refs, n_w, n_extra, nk, n_valid, epilogue):
    x_ref = refs[0]
    w_refs = refs[1:1 + n_w]
    e_refs = refs[1 + n_w:1 + n_w + n_extra]
    o_refs = refs[1 + n_w + n_extra:]

    def parts():
        x = x_ref[...]
        return [_dot(x, w[...].astype(BF16)) for w in w_refs]

    if n_valid is not None:
        j = pl.program_id(1)

        @pl.when(j < n_valid)
        def _():
            epilogue(parts, e_refs, o_refs)

        @pl.when(j >= n_valid)
        def _():
            for o in o_refs:
                o[...] = jnp.zeros(o.shape, o.dtype)
        return

    if nk == 1:
        epilogue(parts, e_refs, o_refs)
        return

    k = pl.program_id(2)
    acc = o_refs[0]

    @pl.when(k == 0)
    def _():
        acc[...] = parts()[0]

    if nk > 2:
        @pl.when(jnp.logical_and(k > 0, k < nk - 1))
        def _():
            acc[...] += parts()[0]

    @pl.when(k == nk - 1)
    def _():
        epilogue(lambda: [acc[...] + parts()[0]], e_refs, o_refs)


def _matmul(x, ws, extras, outs, epilogue, *, tm, tn, tk, nj, n_valid=None, x_single_buffer=False):
    m, kdim = x.shape
    assert m % tm == 0 and kdim % tk == 0, (x.shape, tm, tk)
    nk = kdim // tk
    if nk > 1:
        assert len(ws) == 1 and outs[0][0].dtype == F32 and n_valid is None
    last = nj - 1 if n_valid is None else n_valid - 1

    def ij(im):
        return lambda i, j, k: im(i, j)

    def w_spec(lead, off):
        return pl.BlockSpec((None,) * len(lead) + (tk, tn), lambda i, j, k: (*lead, k, off + jnp.minimum(j, last)))

    x_mode = {"pipeline_mode": pl.Buffered(1)} if x_single_buffer else {}
    in_specs = [pl.BlockSpec((tm, tk), lambda i, j, k: (i, k), **x_mode)]
    in_specs += [w_spec(lead, off) for _, lead, off in ws]
    in_specs += [pl.BlockSpec(blk, ij(im)) for _, blk, im in extras]
    out_specs = [pl.BlockSpec(blk, ij(im)) for _, blk, im in outs]
    streamed = sum(_nbytes((tk, tn), w.dtype) for w, _, _ in ws)
    streamed += sum(_nbytes(blk, a.dtype) for a, blk, _ in extras) + sum(_nbytes(blk, s.dtype) for s, blk, _ in outs)
    x_bytes = _nbytes((tm, tk), x.dtype)
    blocks = streamed + (x_bytes if not x_single_buffer else x_bytes // 2)
    temps = (len(ws) + 1) * _nbytes((tm, tn), F32) + sum(_nbytes((tk, tn), BF16) for w, _, _ in ws if w.dtype != BF16)
    return pl.pallas_call(
        functools.partial(_mm_body, n_w=len(ws), n_extra=len(extras), nk=nk, n_valid=n_valid, epilogue=epilogue),
        grid=(m // tm, nj, nk),
        in_specs=in_specs,
        out_specs=out_specs,
        out_shape=[s for s, _, _ in outs],
        compiler_params=_cparams(("parallel", "parallel", "arbitrary"), blocks, temps),
    )(x, *[w for w, _, _ in ws], *[a for a, _, _ in extras])


def _ep_store(parts, e_refs, o_refs):
    o_refs[0][...] = parts()[0].astype(o_refs[0].dtype)


def _ep_swiglu(parts, e_refs, o_refs):
    g, u = parts()
    o_refs[0][...] = (g * _sigmoid(g) * u).astype(o_refs[0].dtype)


def _ep_residual(parts, e_refs, o_refs, *, alpha, gscale):
    x_ref, m_ref = e_refs
    y = parts()[0]
    nb = m_ref.shape[0]
    rows = y.shape[0] // nb
    for r in range(nb):
        sl = slice(r * rows, (r + 1) * rows)
        o_refs[0][sl, :] = alpha * x_ref[sl, :] + (gscale * m_ref[r]) * y[sl, :]


def _ep_headnorm(parts, e_refs, o_refs, *, n_norm_blocks):
    g_ref = e_refs[0]
    j = pl.program_id(1)

    @pl.when(j < n_norm_blocks)
    def _():
        y = parts()[0]
        for h in range(y.shape[1] // HEAD_DIM):
            sl = slice(h * HEAD_DIM, (h + 1) * HEAD_DIM)
            yh = y[:, sl]
            r = lax.rsqrt(jnp.mean(yh * yh, axis=-1, keepdims=True) + RMS_EPS)
            o_refs[0][:, sl] = yh * r * g_ref[0:1, sl]

    @pl.when(j >= n_norm_blocks)
    def _():
        o_refs[0][...] = parts()[0]


def _ep_rownorm(parts, e_refs, o_refs):
    y = parts()[0]
    r = lax.rsqrt(jnp.mean(y * y, axis=-1, keepdims=True) + RMS_EPS)
    o_refs[0][...] = (y * r * e_refs[0][0:1, :]).astype(o_refs[0].dtype)


def _ep_kva(parts, e_refs, o_refs, *, lora):
    y = parts()[0]
    c = y[:, :lora]
    r = lax.rsqrt(jnp.mean(c * c, axis=-1, keepdims=True) + RMS_EPS)
    o_refs[0][...] = c * r * e_refs[0][0:1, :]
    o_refs[1][...] = y[:, lora:]


def _mm_plain(x, w, out_dtype, *, tm=TM, tn=TN):
    m, kdim = x.shape
    n = w[0].shape[-1]
    tm, tn = _tile(m, tm, SUBLANES), _tile(n, tn)
    out = (jax.ShapeDtypeStruct((m, n), out_dtype), (tm, tn), lambda i, j: (i, j))
    return _matmul(x, [(*w, 0)], [], [out], _ep_store, tm=tm, tn=tn, tk=kdim, nj=n // tn)[0]


def _mm_swiglu(h, w_gu, f_pad):
    m, d = h.shape
    d_ff = w_gu[0].shape[-1] // 2
    tm, tn = _tile(m, TM_WIDE, SUBLANES), _tile(d_ff, TN_FF)
    assert f_pad % tn == 0
    n_valid = d_ff // tn
    out = (jax.ShapeDtypeStruct((m, f_pad), BF16), (tm, tn), lambda i, j: (i, j))
    return _matmul(h, [(*w_gu, 0), (*w_gu, n_valid)], [], [out], _ep_swiglu, tm=tm, tn=tn, tk=d, nj=f_pad // tn,
                   n_valid=n_valid, x_single_buffer=True)[0]


def _cast_pad_body(w_ref, o_ref, *, n_valid):
    r = pl.program_id(1)

    @pl.when(r < n_valid)
    def _():
        o_ref[...] = w_ref[...].astype(o_ref.dtype)

    @pl.when(r >= n_valid)
    def _():
        o_ref[...] = jnp.zeros(o_ref.shape, o_ref.dtype)


def _cast_pad_rows(w, rows_pad):
    g, rows, cols = w.shape
    tr = _tile(rows, TR_LN)
    assert rows_pad % tr == 0
    n_valid = rows // tr
    return pl.pallas_call(
        functools.partial(_cast_pad_body, n_valid=n_valid),
        grid=(g, rows_pad // tr),
        in_specs=[pl.BlockSpec((None, tr, cols), lambda l, r: (l, jnp.minimum(r, n_valid - 1), 0))],
        out_specs=pl.BlockSpec((None, tr, cols), lambda l, r: (l, r, 0)),
        out_shape=jax.ShapeDtypeStruct((g, rows_pad, cols), BF16),
        compiler_params=_cparams(("parallel", "parallel"), _nbytes((tr, cols), F32) + _nbytes((tr, cols), BF16)),
    )(w)


def _mm_residual(a, w, x, mod, stream, chunk, gscale, alpha):
    m, kdim = a.shape
    d = w[0].shape[-1]
    tm, tn, tk = _tile(m, TM_WIDE, SUBLANES), _tile(d, TN), _tile(kdim, TK)
    if tm > stream.rows_per_mod and tm % stream.rows_per_mod:
        tm = stream.rows_per_mod
    mblk, mim = stream.mod_block(tm, tn, chunk * (d // tn))
    extras = [(x, (tm, tn), lambda i, j: (i, j)), (mod, mblk, mim)]
    out = (jax.ShapeDtypeStruct((m, d), F32), (tm, tn), lambda i, j: (i, j))
    ep = functools.partial(_ep_residual, alpha=alpha, gscale=gscale)
    return _matmul(a, [(*w, 0)], extras, [out], ep, tm=tm, tn=tn, tk=tk, nj=d // tn)[0]


def _ln_body(*refs, with_h):
    if with_h:
        z_ref, g_ref, b_ref, sh_ref, sc_ref, x_ref, h_ref = refs
    else:
        z_ref, g_ref, b_ref, x_ref = refs
    z = z_ref[...]
    zc = z - jnp.mean(z, axis=-1, keepdims=True)
    y = zc * lax.rsqrt(jnp.mean(zc * zc, axis=-1, keepdims=True) + LN_EPS)
    y = y * g_ref[...] + b_ref[...]
    x_ref[...] = y
    if with_h:
        h_ref[...] = (y * (1 + sc_ref[0]) + sh_ref[0]).astype(BF16)


def _ln_mod(z, g, b, mod=None, stream=None, shift_chunk=None):
    m, d = z.shape
    tr = _tile(m, TR_LN, SUBLANES)
    with_h = mod is not None
    row = pl.BlockSpec((tr, d), lambda i: (i, 0))
    vec = pl.BlockSpec((1, d), lambda i: (0, 0))
    in_specs, args = [row, vec, vec], [z, g.reshape(1, d), b.reshape(1, d)]
    out_specs, out_shape = [row], [jax.ShapeDtypeStruct((m, d), F32)]
    blocks = 2 * _nbytes((tr, d), F32)
    if with_h:
        for chunk in (shift_chunk, shift_chunk + 1):
            blk, im = stream.mod_block(tr, d, chunk)
            in_specs.append(pl.BlockSpec(blk, lambda i, im=im: im(i, 0)))
            args.append(mod)
        out_specs.append(row)
        out_shape.append(jax.ShapeDtypeStruct((m, d), BF16))
        blocks += _nbytes((tr, d), BF16)
    res = pl.pallas_call(
        functools.partial(_ln_body, with_h=with_h),
        grid=(m // tr,), in_specs=in_specs, out_specs=out_specs, out_shape=out_shape,
        compiler_params=_cparams(("parallel",), blocks, 2 * _nbytes((tr, d), F32)),
    )(*args)
    return (res[0], res[1]) if with_h else (res[0], None)


def _modulate_body(x_ref, sh_ref, sc_ref, h_ref):
    h_ref[...] = (x_ref[...] * (1 + sc_ref[0]) + sh_ref[0]).astype(BF16)


def _modulate(x, mod, stream, shift_chunk):
    m, d = x.shape
    tr = _tile(m, TR_LN, SUBLANES)
    row = pl.BlockSpec((tr, d), lambda i: (i, 0))
    in_specs, args = [row], [x]
    for chunk in (shift_chunk, shift_chunk + 1):
        blk, im = stream.mod_block(tr, d, chunk)
        in_specs.append(pl.BlockSpec(blk, lambda i, im=im: im(i, 0)))
        args.append(mod)
    return pl.pallas_call(
        _modulate_body, grid=(m // tr,), in_specs=in_specs, out_specs=row,
        out_shape=jax.ShapeDtypeStruct((m, d), BF16),
        compiler_params=_cparams(("parallel",), _nbytes((tr, d), F32) + _nbytes((tr, d), BF16)),
    )(*args)


def _rope_tables(n_tok, rot_dim):
    n_rows = n_tok // GRID_W
    row = jnp.repeat(jnp.arange(n_rows, dtype=F32), GRID_W)
    col = jnp.tile(jnp.arange(GRID_W, dtype=F32), n_rows)
    quarter = rot_dim // 4
    inv_freq = ROPE_THETA ** (-jnp.arange(quarter, dtype=F32) / quarter)
    ang_r, ang_c = row[:, None] * inv_freq, col[:, None] * inv_freq
    cos = jnp.concatenate([jnp.cos(ang_r)] * 2 + [jnp.cos(ang_c)] * 2, axis=-1)
    sin = jnp.concatenate([-jnp.sin(ang_r), jnp.sin(ang_r), -jnp.sin(ang_c), jnp.sin(ang_c)], axis=-1)
    reps = LANES // rot_dim
    return jnp.tile(cos, (1, reps)), jnp.tile(sin, (1, reps))


def _rope(x, cos, sin, quarter):
    lane = lax.broadcasted_iota(jnp.int32, x.shape, x.ndim - 1)
    first = (lane & quarter) == 0
    partner = jnp.where(first, pltpu.roll(x, LANES - quarter, axis=x.ndim - 1), pltpu.roll(x, quarter, axis=x.ndim - 1))
    return x * cos + partner * sin


def _softmax_pv(s_loc, v_loc, s_ctx, v_ctx, sink):
    g, tq, _ = s_loc.shape
    m = jnp.max(s_loc, axis=-1, keepdims=True)
    if s_ctx is not None:
        m = jnp.maximum(m, jnp.max(s_ctx, axis=-1, keepdims=True))
    if sink is not None:
        m = jnp.maximum(m, sink)
    p = jnp.exp2(s_loc - m)
    l = jnp.sum(p, axis=-1, keepdims=True)
    o = _dot(p.astype(BF16).reshape(g * tq, -1), v_loc)
    if s_ctx is not None:
        p = jnp.exp2(s_ctx - m)
        l = l + jnp.sum(p, axis=-1, keepdims=True)
        o = o + _dot(p.astype(BF16).reshape(g * tq, -1), v_ctx)
    if sink is not None:
        l = l + jnp.exp2(sink - m)
    return o.reshape(g, tq, -1) / l


def _gqa_body(*refs, seq, n_ctx, groups, hps, tq, rope, banded, has_sink, scale):
    it = iter(refs)
    q_ref, k_ref, v_ref = next(it), next(it), next(it)
    kc_ref, vc_ref = (next(it), next(it)) if n_ctx else (None, None)
    cos_ref, sin_ref = (next(it), next(it)) if rope else (None, None)
    sink_ref = next(it) if has_sink else None
    o_ref = next(it)
    quarter = HEAD_DIM // 4
    qw = groups * HEAD_DIM

    for hh in range(hps):
        hd = slice(hh * HEAD_DIM, (hh + 1) * HEAD_DIM)
        k = k_ref[:, hd]
        if rope:
            k = _rope(k, cos_ref[...], sin_ref[...], quarter)
        kb, vb = k.astype(BF16), v_ref[:, hd].astype(BF16)
        kcb, vcb = (kc_ref[:, hd].astype(BF16), vc_ref[:, hd].astype(BF16)) if n_ctx else (None, None)
        sink = sink_ref[hh, :, 0:1, 0:1] if has_sink else None

        for qb in range(seq // tq):
            r0 = qb * tq
            q_raw, q_rot = [], []
            for g in range(groups):
                q = q_ref[r0:r0 + tq, hh * qw + g * HEAD_DIM:hh * qw + (g + 1) * HEAD_DIM] * scale
                q_raw.append(q.astype(BF16))
                if rope:
                    q_rot.append(_rope(q, cos_ref[r0:r0 + tq, :], sin_ref[r0:r0 + tq, :], quarter).astype(BF16))
            q_raw = jnp.concatenate(q_raw, axis=0)
            q_loc = jnp.concatenate(q_rot, axis=0) if rope else q_raw
            k0, k1 = (max(0, r0 - WINDOW), min(seq, r0 + tq + WINDOW)) if banded else (0, seq)
            s_loc = _dot_nt(q_loc, kb[k0:k1]).reshape(groups, tq, k1 - k0)
            if banded:
                q_pos = r0 + lax.broadcasted_iota(jnp.int32, (tq, k1 - k0), 0)
                k_pos = k0 + lax.broadcasted_iota(jnp.int32, (tq, k1 - k0), 1)
                s_loc = jnp.where((jnp.abs(q_pos - k_pos) <= WINDOW)[None], s_loc, NEG_INF)
            s_ctx = _dot_nt(q_raw, kcb).reshape(groups, tq, n_ctx) if n_ctx else None
            o = _softmax_pv(s_loc, vb[k0:k1], s_ctx, vcb, sink)
            for g in range(groups):
                o_ref[r0:r0 + tq, hh * qw + g * HEAD_DIM:hh * qw + (g + 1) * HEAD_DIM] = o[g].astype(o_ref.dtype)


def _gqa_attention(qkv, n_batch, seq, n_q_heads, *, ctx_kv=None, tabs=None, sink=None, banded=False):
    groups = n_q_heads // N_KV_HEADS
    tq = _tile(seq, TQ_GQA, SUBLANES)
    hps = max(1, min(N_KV_HEADS, ATTN_ROWS_PER_STEP // seq))
    assert N_KV_HEADS % hps == 0
    qw, kw = hps * groups * HEAD_DIM, hps * HEAD_DIM
    k_blk0, v_blk0 = n_q_heads // hps, (n_q_heads + N_KV_HEADS) // hps
    in_specs = [pl.BlockSpec((seq, qw), lambda b, j: (b, j)),
                pl.BlockSpec((seq, kw), lambda b, j: (b, k_blk0 + j)),
                pl.BlockSpec((seq, kw), lambda b, j: (b, v_blk0 + j))]
    args = [qkv, qkv, qkv]
    blocks = _nbytes((seq, qw + 2 * kw), F32) + _nbytes((seq, qw), BF16)
    n_ctx = 0
    if ctx_kv is not None:
        n_ctx = ctx_kv[0].shape[1]
        in_specs += [pl.BlockSpec((None, n_ctx, kw), lambda b, j: (b, 0, j))] * 2
        args += list(ctx_kv)
        blocks += 2 * _nbytes((n_ctx, kw), F32)
    if tabs is not None:
        in_specs += [pl.BlockSpec((seq, LANES), lambda b, j: (0, 0))] * 2
        args += list(tabs)
        blocks += 2 * _nbytes((seq, LANES), F32)
    if sink is not None:
        sink_b = jnp.broadcast_to((sink.astype(F32) * LOG2E).reshape(N_KV_HEADS, groups, 1, 1),
                                  (N_KV_HEADS, groups, SUBLANES, LANES))
        in_specs.append(pl.BlockSpec((hps, groups, SUBLANES, LANES), lambda b, j: (j, 0, 0, 0)))
        args.append(sink_b)
        blocks += _nbytes((hps, groups, SUBLANES, LANES), F32)
    body = functools.partial(_gqa_body, seq=seq, n_ctx=n_ctx, groups=groups, hps=hps, tq=tq, rope=tabs is not None,
                             banded=banded, has_sink=sink is not None, scale=HEAD_DIM ** -0.5 * LOG2E)
    temps = 3 * _nbytes((groups * tq, seq + n_ctx), F32)
    return pl.pallas_call(
        body, grid=(n_batch, N_KV_HEADS // hps), in_specs=in_specs,
        out_specs=pl.BlockSpec((seq, qw), lambda b, j: (b, j)),
        out_shape=jax.ShapeDtypeStruct((n_batch * seq, n_q_heads * HEAD_DIM), BF16),
        compiler_params=_cparams(("parallel", "parallel"), blocks, temps),
    )(*args)


def _mla_body(*refs, seq, n_ctx, pps, tq, rope, scale):
    it = iter(refs)
    qn_ref, qr_ref, c_ref, kr_ref, w_ref = next(it), next(it), next(it), next(it), next(it)
    cc_ref, krc_ref = (next(it), next(it)) if n_ctx else (None, None)
    cos_ref, sin_ref = (next(it), next(it)) if rope else (None, None)
    o_ref = next(it)
    quarter = MLA_ROPE // 4
    hw = MLA_NOPE + MLA_V

    def rope_keys(kr):
        return kr.astype(BF16), pltpu.roll(kr, MLA_ROPE, axis=1).astype(BF16)

    w = w_ref[...]
    kv = _dot(c_ref[...].astype(BF16), w)
    kr = kr_ref[...]
    if rope:
        kr = _rope(kr, cos_ref[...], sin_ref[...], quarter)
    kr_pair = rope_keys(kr)
    if n_ctx:
        kvc = _dot(cc_ref[...].astype(BF16), w)
        krc_pair = rope_keys(krc_ref[...])

    for pp in range(pps):
        for qb in range(seq // tq):
            rows = slice(qb * tq, (qb + 1) * tq)
            qr = qr_ref[rows, pp * LANES:(pp + 1) * LANES] * scale
            qr_raw = qr.astype(BF16)
            qr_loc = _rope(qr, cos_ref[rows, :], sin_ref[rows, :], quarter).astype(BF16) if rope else qr_raw
            for h in range(2):
                hd = 2 * pp + h
                qn = (qn_ref[rows, hd * MLA_NOPE:(hd + 1) * MLA_NOPE] * scale).astype(BF16)
                kn = kv[:, hd * hw:hd * hw + MLA_NOPE].astype(BF16)
                vv = kv[:, hd * hw + MLA_NOPE:(hd + 1) * hw].astype(BF16)
                s_loc = (_dot_nt(qn, kn) + _dot_nt(qr_loc, kr_pair[h]))[None]
                s_ctx = vc = None
                if n_ctx:
                    knc = kvc[:, hd * hw:hd * hw + MLA_NOPE].astype(BF16)
                    vc = kvc[:, hd * hw + MLA_NOPE:(hd + 1) * hw].astype(BF16)
                    s_ctx = (_dot_nt(qn, knc) + _dot_nt(qr_raw, krc_pair[h]))[None]
                o = _softmax_pv(s_loc, vv, s_ctx, vc, None)
                o_ref[rows, hd * MLA_V:(hd + 1) * MLA_V] = o[0].astype(o_ref.dtype)


def _mla_attention(q, ckv, krope, w_kv_b, n_batch, seq, n_heads, *, ctx=None, tabs=None):
    lora = ckv.shape[1]
    tq = _tile(seq, TQ_MLA, SUBLANES)
    n_pairs = n_heads // 2
    pps = max(1, min(n_pairs, ATTN_ROWS_PER_STEP // seq))
    assert n_pairs % pps == 0
    w_arr, w_lead = w_kv_b
    pw = 2 * (MLA_NOPE + MLA_V)
    in_specs = [pl.BlockSpec((seq, pps * 2 * MLA_NOPE), lambda b, p: (b, p)),
                pl.BlockSpec((seq, pps * LANES), lambda b, p: (b, n_pairs * 2 * MLA_NOPE // (pps * LANES) + p)),
                pl.BlockSpec((seq, lora), lambda b, p: (b, 0)),
                pl.BlockSpec((seq, LANES), lambda b, p: (b, 0)),
                pl.BlockSpec((None,) * len(w_lead) + (lora, pps * pw), lambda b, p: (*w_lead, 0, p))]
    args = [q, q, ckv, krope, w_arr]
    blocks = _nbytes((seq, pps * (2 * MLA_NOPE + LANES) + LANES + lora), F32) + _nbytes((lora, pps * pw), BF16)
    blocks += _nbytes((seq, pps * 2 * MLA_V), BF16)
    n_ctx = 0
    if ctx is not None:
        n_ctx = ctx[0].shape[1]
        in_specs += [pl.BlockSpec((None, n_ctx, lora), lambda b, p: (b, 0, 0)),
                     pl.BlockSpec((None, n_ctx, LANES), lambda b, p: (b, 0, 0))]
        args += list(ctx)
        blocks += _nbytes((n_ctx, lora + LANES), F32)
    if tabs is not None:
        in_specs += [pl.BlockSpec((seq, LANES), lambda b, p: (0, 0))] * 2
        args += list(tabs)
        blocks += 2 * _nbytes((seq, LANES), F32)
    body = functools.partial(_mla_body, seq=seq, n_ctx=n_ctx, pps=pps, tq=tq, rope=tabs is not None,
                             scale=(MLA_NOPE + MLA_ROPE) ** -0.5 * LOG2E)
    temps = 3 * _nbytes((tq, seq + n_ctx), F32) + 2 * _nbytes((seq + n_ctx, pps * pw), F32)
    return pl.pallas_call(
        body, grid=(n_batch, n_pairs // pps), in_specs=in_specs,
        out_specs=pl.BlockSpec((seq, pps * 2 * MLA_V), lambda b, p: (b, p)),
        out_shape=jax.ShapeDtypeStruct((n_batch * seq, n_heads * MLA_V), BF16),
        compiler_params=_cparams(("parallel", "parallel"), blocks, temps),
    )(*args)


def _mixer_gqa(h, x, mod, stream, n_batch, alpha, w_qkv, w_o, gain, *, ctx_kv, tabs, sink, banded):
    n_q_heads = w_o[0].shape[-2] // HEAD_DIM
    kvw = N_KV_HEADS * HEAD_DIM
    m, d = h.shape
    n = w_qkv[0].shape[-1]
    tm, tn = _tile(m, TM, SUBLANES), kvw
    out = (jax.ShapeDtypeStruct((m, n), F32), (tm, tn), lambda i, j: (i, j))
    if gain is None:
        qkv = _matmul(h, [(*w_qkv, 0)], [], [out], _ep_store, tm=tm, tn=tn, tk=d, nj=n // tn)[0]
    else:
        ep = functools.partial(_ep_headnorm, n_norm_blocks=(n - kvw) // tn)
        extras = [(gain, (SUBLANES, tn), lambda i, j: (0, j))]
        qkv = _matmul(h, [(*w_qkv, 0)], extras, [out], ep, tm=tm, tn=tn, tk=d, nj=n // tn)[0]
    o = _gqa_attention(qkv, n_batch, stream.seq, n_q_heads, ctx_kv=ctx_kv, tabs=tabs, sink=sink, banded=banded)
    z = _mm_residual(o, w_o, x, mod, stream, 5, 1.0, alpha)
    return z, qkv


def _mixer_mla(h, x, mod, stream, n_batch, alpha, p, *, ctx, tabs):
    n_heads = p["w_o"][0].shape[-2] // MLA_V
    m, d = h.shape
    lora = p["kv_gain"].shape[1]
    q_lora = p["w_q_a"][0].shape[-1]
    tm = _tile(m, TM, SUBLANES)
    out = (jax.ShapeDtypeStruct((m, q_lora), BF16), (tm, q_lora), lambda i, j: (i, 0))
    qa = _matmul(h, [(*p["w_q_a"], 0)], [(p["q_gain"], (SUBLANES, q_lora), lambda i, j: (0, 0))], [out], _ep_rownorm,
                 tm=tm, tn=q_lora, tk=d, nj=1)[0]
    q = _mm_plain(qa, p["w_q_b"], F32, tm=TM, tn=2 * TN)
    kvw = p["w_kv_a"][0].shape[-1]
    outs = [(jax.ShapeDtypeStruct((m, lora), F32), (tm, lora), lambda i, j: (i, 0)),
            (jax.ShapeDtypeStruct((m, LANES), F32), (tm, LANES), lambda i, j: (i, 0))]
    ckv, krope = _matmul(h, [(*p["w_kv_a"], 0)], [(p["kv_gain"], (SUBLANES, lora), lambda i, j: (0, 0))], outs,
                         functools.partial(_ep_kva, lora=lora), tm=tm, tn=kvw, tk=d, nj=1)
    o = _mla_attention(q, ckv, krope, p["w_kv_b"], n_batch, stream.seq, n_heads, ctx=ctx, tabs=tabs)
    z = _mm_residual(o, p["w_o"], x, mod, stream, 5, 1.0, alpha)
    return z, ckv, krope


def _pad_cols(w, n):
    return jnp.pad(w, [(0, 0)] * (w.ndim - 1) + [(0, n - w.shape[-1])])


def kernel(x_prompt, x_sample, cache_k, cache_v, cache_ckv, cache_krope, c, c_ctx, w_ada, b_ada, ln_g, ln_b, ffn_w_gu, ffn_w_dn, a_w_qkv, a_w_o, a_q_norm, a_k_norm, b_w_qkv, b_w_o, b_sink, c_w_q_a, c_q_a_norm, c_w_q_b, c_w_kv_a, c_kv_norm, c_w_kv_b, c_w_o):
    n_ctx_b, seq_ctx, d = x_prompt.shape
    n_lat_b, seq_lat, _ = x_sample.shape
    depth = w_ada.shape[0]
    d_ff = ffn_w_dn.shape[2]
    f_pad = _round_up(d_ff, TK)
    alpha = (2 * depth) ** 0.25
    n_q_heads = a_w_o.shape[1] // HEAD_DIM
    kvw = N_KV_HEADS * HEAD_DIM
    past = cache_k.shape[2]

    streams = (_Stream(n_ctx_b * seq_ctx, seq_ctx, n_ctx_b * seq_ctx, n_lat_b), _Stream(n_lat_b * seq_lat, seq_lat, seq_lat, 0))
    n_batches = (n_ctx_b, n_lat_b)
    n_mod_rows = _round_up(n_lat_b + 1, 2 * SUBLANES)
    cvec = jnp.zeros((n_mod_rows, d), F32).at[:n_lat_b].set(c).at[n_lat_b].set(c_ctx)
    mod_all = _adaln(cvec, w_ada, b_ada).reshape(depth, n_mod_rows, 1, N_MOD * d)

    w_dn = _cast_pad_rows(ffn_w_dn.reshape(depth * 2, d_ff, d), f_pad)
    a_qkv, a_o, b_qkv, b_o = (w.astype(BF16) for w in (a_w_qkv, a_w_o, b_w_qkv, b_w_o))
    c_q_a, c_kv_b, c_o = (w.astype(BF16) for w in (c_w_q_a, c_w_kv_b, c_w_o))

    tabs_gqa = _rope_tables(seq_lat, HEAD_DIM)
    tabs_mla = _rope_tables(seq_lat, MLA_ROPE)

    xs = [x_prompt.reshape(-1, d), x_sample.reshape(-1, d)]
    hs = [_modulate(xs[s], mod_all[0], streams[s], 0) for s in range(2)]
    new_k, new_v, new_ckv, new_krope = [], [], [], []

    for i in range(depth):
        kind, idx = i % N_MIXERS, i // N_MIXERS
        mod = mod_all[i]

        def ffn(s, which, next_mod, next_chunk):
            act = _mm_swiglu(hs[s], (ffn_w_gu, (i, which)), f_pad)
            z = _mm_residual(act, (w_dn, (2 * i + which,)), xs[s], mod, streams[s], 2 + 6 * which, 0.5, alpha)
            xs[s], hs[s] = _ln_mod(z, ln_g[i, 2 * which], ln_b[i, 2 * which], next_mod, streams[s], next_chunk)

        for s in range(2):
            ffn(s, 0, mod, 3)

        if kind in (0, 1):
            slot = i - i // N_MIXERS
            if kind == 0:
                w_qkv, w_o = (a_qkv, (idx,)), (a_o, (idx,))
                gain = jnp.concatenate([jnp.tile(a_q_norm[idx], n_q_heads), jnp.tile(a_k_norm[idx], N_KV_HEADS),
                                        jnp.ones((kvw,), F32)])
                gain = jnp.broadcast_to(gain[None], (SUBLANES, gain.shape[0]))
                sink = None
            else:
                w_qkv, w_o = (b_qkv, (idx,)), (b_o, (idx,))
                gain, sink = None, b_sink[idx]
            ctx_kv = (cache_k[:, slot].reshape(n_lat_b, past, kvw), cache_v[:, slot].reshape(n_lat_b, past, kvw))
            for s in range(2):
                lat = s == 1
                z, qkv = _mixer_gqa(hs[s], xs[s], mod, streams[s], n_batches[s], alpha, w_qkv, w_o, gain,
                                    ctx_kv=ctx_kv if lat else None, tabs=tabs_gqa if lat else None,
                                    sink=sink, banded=lat and kind == 1)
                xs[s], hs[s] = _ln_mod(z, ln_g[i, 1], ln_b[i, 1], mod, streams[s], 6)
                if not lat:
                    qw = n_q_heads * HEAD_DIM
                    new_k.append(qkv[:, qw:qw + kvw].reshape(n_ctx_b, seq_ctx, N_KV_HEADS, HEAD_DIM))
                    new_v.append(qkv[:, qw + kvw:].reshape(n_ctx_b, seq_ctx, N_KV_HEADS, HEAD_DIM))
        else:
            n_heads = c_w_o.shape[1] // MLA_V
            lora = c_kv_norm.shape[1]
            w_q_b = c_w_q_b[idx].astype(BF16).reshape(-1, n_heads, MLA_NOPE + MLA_ROPE)
            p = {
                "w_q_a": (c_q_a, (idx,)),
                "q_gain": jnp.broadcast_to(c_q_a_norm[idx][None], (SUBLANES, c_q_a_norm.shape[1])),
                "w_q_b": (jnp.concatenate([w_q_b[:, :, :MLA_NOPE].reshape(-1, n_heads * MLA_NOPE),
                                           w_q_b[:, :, MLA_NOPE:].reshape(-1, n_heads * MLA_ROPE)], axis=1), ()),
                "w_kv_a": (_pad_cols(c_w_kv_a[idx].astype(BF16), lora + LANES), ()),
                "kv_gain": jnp.broadcast_to(c_kv_norm[idx][None], (SUBLANES, lora)),
                "w_kv_b": (c_kv_b, (idx,)),
                "w_o": (c_o, (idx,)),
            }
            ctx = (cache_ckv[:, idx], _pad_cols(cache_krope[:, idx], LANES))
            for s in range(2):
                lat = s == 1
                z, ckv, krope = _mixer_mla(hs[s], xs[s], mod, streams[s], n_batches[s], alpha, p,
                                           ctx=ctx if lat else None, tabs=tabs_mla if lat else None)
                xs[s], hs[s] = _ln_mod(z, ln_g[i, 1], ln_b[i, 1], mod, streams[s], 6)
                if not lat:
                    new_ckv.append(ckv.reshape(n_ctx_b, seq_ctx, lora))
                    new_krope.append(krope[:, :MLA_ROPE].reshape(n_ctx_b, seq_ctx, MLA_ROPE))

        last = i == depth - 1
        for s in range(2):
            ffn(s, 1, None if last else mod_all[i + 1], None if last else 0)

    return (xs[0].reshape(x_prompt.shape), xs[1].reshape(x_sample.shape),
            jnp.stack(new_k, axis=1), jnp.stack(new_v, axis=1), jnp.stack(new_ckv, axis=1), jnp.stack(new_krope, axis=1))
```

```python
import functools

import jax
import jax.numpy as jnp
from jax import lax
from jax.experimental import pallas as pl
from jax.experimental.pallas import tpu as pltpu

F32 = jnp.float32
BF16 = jnp.bfloat16

GRID_W = 64
N_MIXERS = 3
HEAD_DIM = 128
N_KV_HEADS = 8
WINDOW = 128
ROPE_THETA = 10000.0
MLA_NOPE = 128
MLA_ROPE = 64
MLA_V = 128
N_MOD = 9
NEG_INF = -1e30
RMS_EPS = 1e-6
LN_EPS = 1e-5

LANES = 128
SUBLANES = 8
VMEM_PHYSICAL_BYTES = 64 * 1024 * 1024
VMEM_CAP_BYTES = VMEM_PHYSICAL_BYTES - 6 * 1024 * 1024

TM_WIDE = 2048
TM = 1024
TN = 1024
TK = 1024
TN_FF = 256
TN_ADA = 512
TR_LN = 256
TQ_GQA = 256
TQ_MLA = 512
ATTN_ROWS_PER_STEP = 1024
P_SLOTS = 2

LOG2E = 1.4426950408889634


def _round_up(n, m):
    return -(-n // m) * m


def _tile(n, pref, unit=LANES):
    if n <= pref:
        return n
    t = (pref // unit) * unit
    while t > unit and n % t:
        t -= unit
    assert n % t == 0, (n, pref)
    return t


def _nbytes(shape, dtype):
    n = jnp.dtype(dtype).itemsize
    for s in shape:
        if s is not None:
            n *= s
    return n


def _cparams(semantics, block_bytes, temp_bytes=0):
    est = 2 * block_bytes + temp_bytes + 8 * 1024 * 1024
    return pltpu.CompilerParams(dimension_semantics=semantics,
                                vmem_limit_bytes=int(min(max(est, 16 * 1024 * 1024), VMEM_CAP_BYTES)))


def _dot(a, b):
    return jnp.dot(a, b, preferred_element_type=F32)


def _dot_nt(a, b):
    return lax.dot_general(a, b, (((1,), (1,)), ((), ())), preferred_element_type=F32)


def _sigmoid(x):
    return 1.0 / (1.0 + jnp.exp(-x))


class _Stream:
    def __init__(self, n_rows, seq, rows_per_mod, mod_row0):
        self.n_rows, self.seq, self.rows_per_mod, self.mod_row0 = n_rows, seq, rows_per_mod, mod_row0

    def mod_block(self, tm, width, col_block):
        rpm, row0 = self.rows_per_mod, self.mod_row0
        if tm > rpm:
            nb = tm // rpm
            assert tm % rpm == 0 and row0 % nb == 0
            return (nb, 1, width), lambda i, j: (row0 // nb + i, 0, col_block + j)
        assert rpm % tm == 0
        return (1, 1, width), lambda i, j: (row0 + (i * tm) // rpm, 0, col_block + j)


def _adaln_body(c_ref, w_ref, b_ref, o_ref):
    cv = c_ref[...]
    s = (cv * _sigmoid(cv)).astype(BF16)
    o_ref[...] = _dot(s, w_ref[...].astype(BF16)) + b_ref[...]


def _adaln(cvec, w_ada, b_ada3, layer):
    _, d, n = w_ada.shape
    r = cvec.shape[0]
    tn = _tile(n, TN_ADA)
    blocks = _nbytes((r, d), F32) + _nbytes((d, tn), F32) + _nbytes((1, tn), F32) + _nbytes((r, tn), F32)
    return pl.pallas_call(
        _adaln_body,
        grid=(n // tn,),
        in_specs=[pl.BlockSpec((r, d), lambda j: (0, 0)),
                  pl.BlockSpec((None, d, tn), lambda j: (layer, 0, j)),
                  pl.BlockSpec((None, 1, tn), lambda j: (layer, 0, j))],
        out_specs=pl.BlockSpec((r, tn), lambda j: (0, j)),
        out_shape=jax.ShapeDtypeStruct((r, n), F32),
        compiler_params=_cparams(("parallel",), blocks, _nbytes((d, tn), BF16)),
    )(cvec, w_ada, b_ada3)


def _mm_body(*refs, n_w, n_extra, nk, n_valid, epilogue, side):
    x_ref = refs[0]
    w_refs = refs[1:1 + n_w]
    e_refs = refs[1 + n_w:1 + n_w + n_extra]
    o_refs = refs[1 + n_w + n_extra:]

    def parts():
        x = x_ref[...]
        return [_dot(x, w[...].astype(BF16)) for w in w_refs]

    if side is not None:
        side_fn, n_in, n_out = side
        side_fn(e_refs[len(e_refs) - n_in:], o_refs[len(o_refs) - n_out:])
        e_refs, o_refs = e_refs[:len(e_refs) - n_in], o_refs[:len(o_refs) - n_out]

    if n_valid is not None:
        j = pl.program_id(1)

        @pl.when(j < n_valid)
        def _():
            epilogue(parts, e_refs, o_refs)

        @pl.when(j >= n_valid)
        def _():
            for o in o_refs:
                o[...] = jnp.zeros(o.shape, o.dtype)
        return

    if nk == 1:
        epilogue(parts, e_refs, o_refs)
        return

    k = pl.program_id(2)
    acc = o_refs[0]

    @pl.when(k == 0)
    def _():
        acc[...] = parts()[0]

    if nk > 2:
        @pl.when(jnp.logical_and(k > 0, k < nk - 1))
        def _():
            acc[...] += parts()[0]

    @pl.when(k == nk - 1)
    def _():
        epilogue(lambda: [acc[...] + parts()[0]], e_refs, o_refs)


def _matmul(x, ws, extras, outs, epilogue, *, tm, tn, tk, nj, n_valid=None, x_single_buffer=False, side=None,
            sequential=False):
    m, kdim = x.shape
    assert m % tm == 0 and kdim % tk == 0, (x.shape, tm, tk)
    nk = kdim // tk
    if nk > 1:
        assert len(ws) == 1 and outs[0][0].dtype == F32 and n_valid is None
    last = nj - 1 if n_valid is None else n_valid - 1

    def ij(im):
        return lambda i, j, k: im(i, j)

    def w_spec(lead, off):
        return pl.BlockSpec((None,) * len(lead) + (tk, tn), lambda i, j, k: (*lead, k, off + jnp.minimum(j, last)))

    x_mode = {"pipeline_mode": pl.Buffered(1)} if x_single_buffer else {}
    in_specs = [pl.BlockSpec((tm, tk), lambda i, j, k: (i, k), **x_mode)]
    in_specs += [w_spec(lead, off) for _, lead, off in ws]
    in_specs += [pl.BlockSpec(blk, ij(im)) for _, blk, im in extras]
    out_specs = [pl.BlockSpec(blk, ij(im)) for _, blk, im in outs]
    streamed = sum(_nbytes((tk, tn), w.dtype) for w, _, _ in ws)
    streamed += sum(_nbytes(blk, a.dtype) for a, blk, _ in extras) + sum(_nbytes(blk, s.dtype) for s, blk, _ in outs)
    x_bytes = _nbytes((tm, tk), x.dtype)
    blocks = streamed + (x_bytes if not x_single_buffer else x_bytes // 2)
    temps = (len(ws) + 1) * _nbytes((tm, tn), F32) + sum(_nbytes((tk, tn), BF16) for w, _, _ in ws if w.dtype != BF16)
    return pl.pallas_call(
        functools.partial(_mm_body, n_w=len(ws), n_extra=len(extras), nk=nk, n_valid=n_valid, epilogue=epilogue, side=side),
        grid=(m // tm, nj, nk),
        in_specs=in_specs,
        out_specs=out_specs,
        out_shape=[s for s, _, _ in outs],
        compiler_params=_cparams(("arbitrary",) * 3 if sequential else ("parallel", "parallel", "arbitrary"), blocks, temps),
    )(x, *[w for w, _, _ in ws], *[a for a, _, _ in extras])


def _ep_store(parts, e_refs, o_refs):
    o_refs[0][...] = parts()[0].astype(o_refs[0].dtype)


def _ep_swiglu(parts, e_refs, o_refs):
    g, u = parts()
    o_refs[0][...] = (g * _sigmoid(g) * u).astype(o_refs[0].dtype)


def _ep_residual(parts, e_refs, o_refs, *, alpha, gscale):
    x_ref, m_ref = e_refs
    y = parts()[0]
    nb = m_ref.shape[0]
    rows = y.shape[0] // nb
    for r in range(nb):
        sl = slice(r * rows, (r + 1) * rows)
        o_refs[0][sl, :] = alpha * x_ref[sl, :] + (gscale * m_ref[r]) * y[sl, :]


def _ep_headnorm(parts, e_refs, o_refs, *, n_norm_blocks):
    g_ref = e_refs[0]
    j = pl.program_id(1)

    @pl.when(j < n_norm_blocks)
    def _():
        y = parts()[0]
        for h in range(y.shape[1] // HEAD_DIM):
            sl = slice(h * HEAD_DIM, (h + 1) * HEAD_DIM)
            yh = y[:, sl]
            r = lax.rsqrt(jnp.mean(yh * yh, axis=-1, keepdims=True) + RMS_EPS)
            o_refs[0][:, sl] = yh * r * g_ref[0:1, sl]

    @pl.when(j >= n_norm_blocks)
    def _():
        o_refs[0][...] = parts()[0]


def _ep_rownorm(parts, e_refs, o_refs):
    y = parts()[0]
    r = lax.rsqrt(jnp.mean(y * y, axis=-1, keepdims=True) + RMS_EPS)
    o_refs[0][...] = (y * r * e_refs[0][0:1, :]).astype(o_refs[0].dtype)


def _ep_kva(parts, e_refs, o_refs, *, lora):
    y = parts()[0]
    c = y[:, :lora]
    r = lax.rsqrt(jnp.mean(c * c, axis=-1, keepdims=True) + RMS_EPS)
    o_refs[0][...] = c * r * e_refs[0][0:1, :]
    o_refs[1][...] = y[:, lora:]


def _mm_plain(x, w, out_dtype, *, tm=TM, tn=TN):
    m, kdim = x.shape
    n = w[0].shape[-1]
    tm, tn = _tile(m, tm, SUBLANES), _tile(n, tn)
    out = (jax.ShapeDtypeStruct((m, n), out_dtype), (tm, tn), lambda i, j: (i, j))
    return _matmul(x, [(*w, 0)], [], [out], _ep_store, tm=tm, tn=tn, tk=kdim, nj=n // tn)[0]


def _mm_swiglu(h, w_gu, f_pad, cast_rows=None, ada=None):
    m, d = h.shape
    d_ff = w_gu[0].shape[-1] // 2
    tm, tn = _tile(m, TM_WIDE, SUBLANES), _tile(d_ff, TN_FF)
    assert f_pad % tn == 0 and (cast_rows is None or ada is None)
    n_valid, nj = d_ff // tn, f_pad // tn
    outs = [(jax.ShapeDtypeStruct((m, f_pad), BF16), (tm, tn), lambda i, j: (i, j))]
    extras, side = [], None
    if ada is not None:
        cvec, w_ada, b_ada3, layer = ada
        n = w_ada.shape[-1]
        ta = _ada_tile(n, (m // tm) * nj)
        n_blocks = n // ta
        blk = lambda i, j: jnp.minimum(i * nj + j, n_blocks - 1)
        extras += [(cvec, cvec.shape, lambda i, j: (0, 0)),
                   (w_ada, (None, d, ta), lambda i, j: (layer, 0, blk(i, j))),
                   (b_ada3, (None, 1, ta), lambda i, j: (layer, 0, blk(i, j)))]
        outs.append((jax.ShapeDtypeStruct((cvec.shape[0], n), F32), (cvec.shape[0], ta), lambda i, j: (0, blk(i, j))))
        side = (functools.partial(_side_adaln, nj=nj, n_blocks=n_blocks), 3, 1)
    if cast_rows is not None:
        w_arr, w_idx = cast_rows
        rows, cols = w_arr.shape[-2:]
        rb = f_pad // ((m // tm) * nj)
        assert _can_cast_alongside(m, d_ff, f_pad), (m, d_ff, f_pad)
        n_row_blocks = rows // rb
        extras.append((w_arr, (None, rb, cols), lambda i, j: (w_idx, jnp.minimum(i * nj + j, n_row_blocks - 1), 0)))
        outs.append((jax.ShapeDtypeStruct((f_pad, cols), BF16), (rb, cols), lambda i, j: (i * nj + j, 0)))
        side = (functools.partial(_side_cast_rows, nj=nj, n_row_blocks=n_row_blocks), 1, 1)
    res = _matmul(h, [(*w_gu, 0), (*w_gu, n_valid)], extras, outs, _ep_swiglu, tm=tm, tn=tn, tk=d, nj=nj,
                  n_valid=n_valid, x_single_buffer=True, side=side, sequential=ada is not None)
    return res if side is not None else res[0]


def _ada_tile(n, steps):
    for ta in range(LANES, n + 1, LANES):
        if n % ta == 0 and n // ta <= steps:
            return ta
    return None


def _side_adaln(e_refs, o_refs, *, nj, n_blocks):
    (c_ref, w_ref, b_ref), (o_ref,) = e_refs, o_refs
    r = pl.program_id(0) * nj + pl.program_id(1)

    @pl.when(r < n_blocks)
    def _():
        _adaln_body(c_ref, w_ref, b_ref, o_ref)


def _swiglu_steps(m, d_ff, f_pad):
    tm, tn = _tile(m, TM_WIDE, SUBLANES), _tile(d_ff, TN_FF)
    return (m // tm) * (f_pad // tn)


def _can_cast_alongside(m, d_ff, f_pad):
    steps = _swiglu_steps(m, d_ff, f_pad)
    rb = f_pad // steps
    return f_pad % steps == 0 and rb % (2 * SUBLANES) == 0 and d_ff % rb == 0


def _side_cast_rows(e_refs, o_refs, *, nj, n_row_blocks):
    (w_ref,), (o_ref,) = e_refs, o_refs
    r = pl.program_id(0) * nj + pl.program_id(1)

    @pl.when(r < n_row_blocks)
    def _():
        o_ref[...] = w_ref[...].astype(o_ref.dtype)

    @pl.when(r >= n_row_blocks)
    def _():
        o_ref[...] = jnp.zeros(o_ref.shape, o_ref.dtype)


def _cast_pad_body(w_ref, o_ref, *, n_valid):
    r = pl.program_id(1)

    @pl.when(r < n_valid)
    def _():
        o_ref[...] = w_ref[...].astype(o_ref.dtype)

    @pl.when(r >= n_valid)
    def _():
        o_ref[...] = jnp.zeros(o_ref.shape, o_ref.dtype)


def _cast_pad_rows(w, rows_pad):
    g, rows, cols = w.shape
    tr = _tile(rows, TR_LN)
    assert rows_pad % tr == 0
    n_valid = rows // tr
    return pl.pallas_call(
        functools.partial(_cast_pad_body, n_valid=n_valid),
        grid=(g, rows_pad // tr),
        in_specs=[pl.BlockSpec((None, tr, cols), lambda l, r: (l, jnp.minimum(r, n_valid - 1), 0))],
        out_specs=pl.BlockSpec((None, tr, cols), lambda l, r: (l, r, 0)),
        out_shape=jax.ShapeDtypeStruct((g, rows_pad, cols), BF16),
        compiler_params=_cparams(("parallel", "parallel"), _nbytes((tr, cols), F32) + _nbytes((tr, cols), BF16)),
    )(w)


def _mm_residual(a, w, x, mod, stream, chunk, gscale, alpha):
    m, kdim = a.shape
    d = w[0].shape[-1]
    tm, tn, tk = _tile(m, TM_WIDE, SUBLANES), _tile(d, TN), _tile(kdim, TK)
    if tm > stream.rows_per_mod and tm % stream.rows_per_mod:
        tm = stream.rows_per_mod
    mblk, mim = stream.mod_block(tm, tn, chunk * (d // tn))
    extras = [(x, (tm, tn), lambda i, j: (i, j)), (mod, mblk, mim)]
    out = (jax.ShapeDtypeStruct((m, d), F32), (tm, tn), lambda i, j: (i, j))
    ep = functools.partial(_ep_residual, alpha=alpha, gscale=gscale)
    return _matmul(a, [(*w, 0)], extras, [out], ep, tm=tm, tn=tn, tk=tk, nj=d // tn)[0]


def _ln_body(*refs, with_h):
    if with_h:
        z_ref, g_ref, b_ref, sh_ref, sc_ref, x_ref, h_ref = refs
    else:
        z_ref, g_ref, b_ref, x_ref = refs
    z = z_ref[...]
    zc = z - jnp.mean(z, axis=-1, keepdims=True)
    y = zc * lax.rsqrt(jnp.mean(zc * zc, axis=-1, keepdims=True) + LN_EPS)
    y = y * g_ref[...] + b_ref[...]
    x_ref[...] = y
    if with_h:
        h_ref[...] = (y * (1 + sc_ref[0]) + sh_ref[0]).astype(BF16)


def _ln_mod(z, g, b, mod=None, stream=None, shift_chunk=None):
    m, d = z.shape
    tr = _tile(m, TR_LN, SUBLANES)
    with_h = mod is not None
    row = pl.BlockSpec((tr, d), lambda i: (i, 0))
    vec = pl.BlockSpec((1, d), lambda i: (0, 0))
    in_specs, args = [row, vec, vec], [z, g.reshape(1, d), b.reshape(1, d)]
    out_specs, out_shape = [row], [jax.ShapeDtypeStruct((m, d), F32)]
    blocks = 2 * _nbytes((tr, d), F32)
    if with_h:
        for chunk in (shift_chunk, shift_chunk + 1):
            blk, im = stream.mod_block(tr, d, chunk)
            in_specs.append(pl.BlockSpec(blk, lambda i, im=im: im(i, 0)))
            args.append(mod)
        out_specs.append(row)
        out_shape.append(jax.ShapeDtypeStruct((m, d), BF16))
        blocks += _nbytes((tr, d), BF16)
    res = pl.pallas_call(
        functools.partial(_ln_body, with_h=with_h),
        grid=(m // tr,), in_specs=in_specs, out_specs=out_specs, out_shape=out_shape,
        compiler_params=_cparams(("parallel",), blocks, 2 * _nbytes((tr, d), F32)),
    )(*args)
    return (res[0], res[1]) if with_h else (res[0], None)


def _modulate_body(x_ref, sh_ref, sc_ref, h_ref):
    h_ref[...] = (x_ref[...] * (1 + sc_ref[0]) + sh_ref[0]).astype(BF16)


def _modulate(x, mod, stream, shift_chunk):
    m, d = x.shape
    tr = _tile(m, TR_LN, SUBLANES)
    row = pl.BlockSpec((tr, d), lambda i: (i, 0))
    in_specs, args = [row], [x]
    for chunk in (shift_chunk, shift_chunk + 1):
        blk, im = stream.mod_block(tr, d, chunk)
        in_specs.append(pl.BlockSpec(blk, lambda i, im=im: im(i, 0)))
        args.append(mod)
    return pl.pallas_call(
        _modulate_body, grid=(m // tr,), in_specs=in_specs, out_specs=row,
        out_shape=jax.ShapeDtypeStruct((m, d), BF16),
        compiler_params=_cparams(("parallel",), _nbytes((tr, d), F32) + _nbytes((tr, d), BF16)),
    )(*args)


def _rope_tables(n_tok, rot_dim):
    n_rows = n_tok // GRID_W
    row = jnp.repeat(jnp.arange(n_rows, dtype=F32), GRID_W)
    col = jnp.tile(jnp.arange(GRID_W, dtype=F32), n_rows)
    quarter = rot_dim // 4
    inv_freq = ROPE_THETA ** (-jnp.arange(quarter, dtype=F32) / quarter)
    ang_r, ang_c = row[:, None] * inv_freq, col[:, None] * inv_freq
    cos = jnp.concatenate([jnp.cos(ang_r)] * 2 + [jnp.cos(ang_c)] * 2, axis=-1)
    sin = jnp.concatenate([-jnp.sin(ang_r), jnp.sin(ang_r), -jnp.sin(ang_c), jnp.sin(ang_c)], axis=-1)
    reps = LANES // rot_dim
    return jnp.tile(cos, (1, reps)), jnp.tile(sin, (1, reps))


def _rope(x, cos, sin, quarter):
    lane = lax.broadcasted_iota(jnp.int32, x.shape, x.ndim - 1)
    first = (lane & quarter) == 0
    partner = jnp.where(first, pltpu.roll(x, LANES - quarter, axis=x.ndim - 1), pltpu.roll(x, quarter, axis=x.ndim - 1))
    return x * cos + partner * sin


def _softmax_pv(s_loc, v_loc, s_ctx, v_ctx, sink):
    g, tq, _ = s_loc.shape
    m = jnp.max(s_loc, axis=-1, keepdims=True)
    if s_ctx is not None:
        m = jnp.maximum(m, jnp.max(s_ctx, axis=-1, keepdims=True))
    if sink is not None:
        m = jnp.maximum(m, sink)
    p = jnp.exp2(s_loc - m)
    l = jnp.sum(p, axis=-1, keepdims=True)
    o = _dot(p.astype(BF16).reshape(g * tq, -1), v_loc)
    if s_ctx is not None:
        p = jnp.exp2(s_ctx - m)
        l = l + jnp.sum(p, axis=-1, keepdims=True)
        o = o + _dot(p.astype(BF16).reshape(g * tq, -1), v_ctx)
    if sink is not None:
        l = l + jnp.exp2(sink - m)
    return o.reshape(g, tq, -1) / l


def _softmax_pv_rows(s_loc, v_loc, s_ctx, v_ctx, p_loc_ref, p_ctx_ref):
    m = jnp.max(s_loc, axis=-1, keepdims=True)
    if s_ctx is not None:
        m = jnp.maximum(m, jnp.max(s_ctx, axis=-1, keepdims=True))
    p = jnp.exp2(s_loc - m)
    l = jnp.sum(p, axis=-1, keepdims=True)
    p_loc_ref[...] = p.astype(p_loc_ref.dtype)
    o = _dot(p_loc_ref[...], v_loc)
    if s_ctx is not None:
        p = jnp.exp2(s_ctx - m)
        l = l + jnp.sum(p, axis=-1, keepdims=True)
        p_ctx_ref[...] = p.astype(p_ctx_ref.dtype)
        o = o + _dot(p_ctx_ref[...], v_ctx)
    return o / l


def _gqa_body(*refs, seq, n_ctx, groups, hps, tq, rope, banded, has_sink, scale):
    it = iter(refs)
    q_ref, k_ref, v_ref = next(it), next(it), next(it)
    kc_ref, vc_ref = (next(it), next(it)) if n_ctx else (None, None)
    cos_ref, sin_ref = (next(it), next(it)) if rope else (None, None)
    sink_ref = next(it) if has_sink else None
    o_ref = next(it)
    quarter = HEAD_DIM // 4
    qw = groups * HEAD_DIM

    for hh in range(hps):
        hd = slice(hh * HEAD_DIM, (hh + 1) * HEAD_DIM)
        k = k_ref[:, hd]
        if rope:
            k = _rope(k, cos_ref[...], sin_ref[...], quarter)
        kb, vb = k.astype(BF16), v_ref[:, hd].astype(BF16)
        kcb, vcb = (kc_ref[:, hd].astype(BF16), vc_ref[:, hd].astype(BF16)) if n_ctx else (None, None)
        sink = sink_ref[hh, :, 0:1, 0:1] if has_sink else None

        for qb in range(seq // tq):
            r0 = qb * tq
            q_raw, q_rot = [], []
            for g in range(groups):
                q = q_ref[r0:r0 + tq, hh * qw + g * HEAD_DIM:hh * qw + (g + 1) * HEAD_DIM] * scale
                q_raw.append(q.astype(BF16))
                if rope:
                    q_rot.append(_rope(q, cos_ref[r0:r0 + tq, :], sin_ref[r0:r0 + tq, :], quarter).astype(BF16))
            q_raw = jnp.concatenate(q_raw, axis=0)
            q_loc = jnp.concatenate(q_rot, axis=0) if rope else q_raw
            k0, k1 = (max(0, r0 - WINDOW), min(seq, r0 + tq + WINDOW)) if banded else (0, seq)
            s_loc = _dot_nt(q_loc, kb[k0:k1]).reshape(groups, tq, k1 - k0)
            if banded:
                q_pos = r0 + lax.broadcasted_iota(jnp.int32, (tq, k1 - k0), 0)
                k_pos = k0 + lax.broadcasted_iota(jnp.int32, (tq, k1 - k0), 1)
                s_loc = jnp.where((jnp.abs(q_pos - k_pos) <= WINDOW)[None], s_loc, NEG_INF)
            s_ctx = _dot_nt(q_raw, kcb).reshape(groups, tq, n_ctx) if n_ctx else None
            o = _softmax_pv(s_loc, vb[k0:k1], s_ctx, vcb, sink)
            for g in range(groups):
                o_ref[r0:r0 + tq, hh * qw + g * HEAD_DIM:hh * qw + (g + 1) * HEAD_DIM] = o[g].astype(o_ref.dtype)


def _gqa_attention(qkv, n_batch, seq, n_q_heads, *, ctx_kv=None, tabs=None, sink=None, banded=False):
    groups = n_q_heads // N_KV_HEADS
    tq = _tile(seq, TQ_GQA, SUBLANES)
    hps = max(1, min(N_KV_HEADS, ATTN_ROWS_PER_STEP // seq))
    assert N_KV_HEADS % hps == 0
    qw, kw = hps * groups * HEAD_DIM, hps * HEAD_DIM
    k_blk0, v_blk0 = n_q_heads // hps, (n_q_heads + N_KV_HEADS) // hps
    in_specs = [pl.BlockSpec((seq, qw), lambda b, j: (b, j)),
                pl.BlockSpec((seq, kw), lambda b, j: (b, k_blk0 + j)),
                pl.BlockSpec((seq, kw), lambda b, j: (b, v_blk0 + j))]
    args = [qkv, qkv, qkv]
    blocks = _nbytes((seq, qw + 2 * kw), F32) + _nbytes((seq, qw), BF16)
    n_ctx = 0
    if ctx_kv is not None:
        n_ctx = ctx_kv[0].shape[1]
        in_specs += [pl.BlockSpec((None, n_ctx, kw), lambda b, j: (b, 0, j))] * 2
        args += list(ctx_kv)
        blocks += 2 * _nbytes((n_ctx, kw), F32)
    if tabs is not None:
        in_specs += [pl.BlockSpec((seq, LANES), lambda b, j: (0, 0))] * 2
        args += list(tabs)
        blocks += 2 * _nbytes((seq, LANES), F32)
    if sink is not None:
        sink_b = jnp.broadcast_to((sink.astype(F32) * LOG2E).reshape(N_KV_HEADS, groups, 1, 1),
                                  (N_KV_HEADS, groups, SUBLANES, LANES))
        in_specs.append(pl.BlockSpec((hps, groups, SUBLANES, LANES), lambda b, j: (j, 0, 0, 0)))
        args.append(sink_b)
        blocks += _nbytes((hps, groups, SUBLANES, LANES), F32)
    body = functools.partial(_gqa_body, seq=seq, n_ctx=n_ctx, groups=groups, hps=hps, tq=tq, rope=tabs is not None,
                             banded=banded, has_sink=sink is not None, scale=HEAD_DIM ** -0.5 * LOG2E)
    temps = 3 * _nbytes((groups * tq, seq + n_ctx), F32)
    return pl.pallas_call(
        body, grid=(n_batch, N_KV_HEADS // hps), in_specs=in_specs,
        out_specs=pl.BlockSpec((seq, qw), lambda b, j: (b, j)),
        out_shape=jax.ShapeDtypeStruct((n_batch * seq, n_q_heads * HEAD_DIM), BF16),
        compiler_params=_cparams(("parallel", "parallel"), blocks, temps),
    )(*args)


def _mla_body(*refs, seq, n_ctx, pps, tq, rope, scale):
    it = iter(refs)
    qn_ref, qr_ref, c_ref, kr_ref, w_ref = next(it), next(it), next(it), next(it), next(it)
    cc_ref, krc_ref = (next(it), next(it)) if n_ctx else (None, None)
    cos_ref, sin_ref = (next(it), next(it)) if rope else (None, None)
    o_ref = next(it)
    p_loc_ref = next(it)
    p_ctx_ref = next(it) if n_ctx else None
    quarter = MLA_ROPE // 4
    hw = MLA_NOPE + MLA_V

    def rope_keys(kr):
        return kr.astype(BF16), pltpu.roll(kr, MLA_ROPE, axis=1).astype(BF16)

    w = w_ref[...]
    kv = _dot(c_ref[...].astype(BF16), w)
    kr = kr_ref[...]
    if rope:
        kr = _rope(kr, cos_ref[...], sin_ref[...], quarter)
    kr_pair = rope_keys(kr)
    if n_ctx:
        kvc = _dot(cc_ref[...].astype(BF16), w)
        krc_pair = rope_keys(krc_ref[...])

    n_qb = seq // tq
    for hd in range(2 * pps):
        pp, h = divmod(hd, 2)
        k_cat = jnp.concatenate([kv[:, hd * hw:hd * hw + MLA_NOPE].astype(BF16), kr_pair[h]], axis=1)
        vv = kv[:, hd * hw + MLA_NOPE:(hd + 1) * hw].astype(BF16)
        kc_cat = vc = None
        if n_ctx:
            kc_cat = jnp.concatenate([kvc[:, hd * hw:hd * hw + MLA_NOPE].astype(BF16), krc_pair[h]], axis=1)
            vc = kvc[:, hd * hw + MLA_NOPE:(hd + 1) * hw].astype(BF16)
        for qb in range(n_qb):
            rows = slice(qb * tq, (qb + 1) * tq)
            slot = (hd * n_qb + qb) % P_SLOTS
            qn = (qn_ref[rows, hd * MLA_NOPE:(hd + 1) * MLA_NOPE] * scale).astype(BF16)
            qr = qr_ref[rows, pp * LANES:(pp + 1) * LANES] * scale
            qr_raw = qr.astype(BF16)
            qr_loc = _rope(qr, cos_ref[rows, :], sin_ref[rows, :], quarter).astype(BF16) if rope else qr_raw
            s_loc = _dot_nt(jnp.concatenate([qn, qr_loc], axis=1), k_cat)
            s_ctx = _dot_nt(jnp.concatenate([qn, qr_raw], axis=1), kc_cat) if n_ctx else None
            o = _softmax_pv_rows(s_loc, vv, s_ctx, vc, p_loc_ref.at[slot], p_ctx_ref.at[slot] if n_ctx else None)
            o_ref[rows, hd * MLA_V:(hd + 1) * MLA_V] = o.astype(o_ref.dtype)


def _mla_attention(q, ckv, krope, w_kv_b, n_batch, seq, n_heads, *, ctx=None, tabs=None):
    lora = ckv.shape[1]
    tq = _tile(seq, TQ_MLA, SUBLANES)
    n_pairs = n_heads // 2
    pps = max(1, min(n_pairs, ATTN_ROWS_PER_STEP // seq))
    assert n_pairs % pps == 0
    w_arr, w_lead = w_kv_b
    pw = 2 * (MLA_NOPE + MLA_V)
    in_specs = [pl.BlockSpec((seq, pps * 2 * MLA_NOPE), lambda b, p: (b, p)),
                pl.BlockSpec((seq, pps * LANES), lambda b, p: (b, n_pairs * 2 * MLA_NOPE // (pps * LANES) + p)),
                pl.BlockSpec((seq, lora), lambda b, p: (b, 0)),
                pl.BlockSpec((seq, LANES), lambda b, p: (b, 0)),
                pl.BlockSpec((None,) * len(w_lead) + (lora, pps * pw), lambda b, p: (*w_lead, 0, p))]
    args = [q, q, ckv, krope, w_arr]
    blocks = _nbytes((seq, pps * (2 * MLA_NOPE + LANES) + LANES + lora), F32) + _nbytes((lora, pps * pw), BF16)
    blocks += _nbytes((seq, pps * 2 * MLA_V), BF16)
    n_ctx = 0
    if ctx is not None:
        n_ctx = ctx[0].shape[1]
        in_specs += [pl.BlockSpec((None, n_ctx, lora), lambda b, p: (b, 0, 0)),
                     pl.BlockSpec((None, n_ctx, LANES), lambda b, p: (b, 0, 0))]
        args += list(ctx)
        blocks += _nbytes((n_ctx, lora + LANES), F32)
    if tabs is not None:
        in_specs += [pl.BlockSpec((seq, LANES), lambda b, p: (0, 0))] * 2
        args += list(tabs)
        blocks += 2 * _nbytes((seq, LANES), F32)
    body = functools.partial(_mla_body, seq=seq, n_ctx=n_ctx, pps=pps, tq=tq, rope=tabs is not None,
                             scale=(MLA_NOPE + MLA_ROPE) ** -0.5 * LOG2E)
    scratch = [pltpu.VMEM((P_SLOTS, tq, seq), BF16)]
    if n_ctx:
        scratch.append(pltpu.VMEM((P_SLOTS, tq, n_ctx), BF16))
    temps = 3 * _nbytes((tq, seq + n_ctx), F32) + 2 * _nbytes((seq + n_ctx, pps * pw), F32)
    temps += _nbytes((P_SLOTS, tq, seq + n_ctx), BF16)
    return pl.pallas_call(
        body, grid=(n_batch, n_pairs // pps), in_specs=in_specs, scratch_shapes=scratch,
        out_specs=pl.BlockSpec((seq, pps * 2 * MLA_V), lambda b, p: (b, p)),
        out_shape=jax.ShapeDtypeStruct((n_batch * seq, n_heads * MLA_V), BF16),
        compiler_params=_cparams(("parallel", "parallel"), blocks, temps),
    )(*args)


def _mixer_gqa(h, x, mod, stream, n_batch, alpha, w_qkv, w_o, gain, *, ctx_kv, tabs, sink, banded):
    n_q_heads = w_o[0].shape[-2] // HEAD_DIM
    kvw = N_KV_HEADS * HEAD_DIM
    m, d = h.shape
    n = w_qkv[0].shape[-1]
    tm, tn = _tile(m, TM, SUBLANES), kvw
    out = (jax.ShapeDtypeStruct((m, n), F32), (tm, tn), lambda i, j: (i, j))
    if gain is None:
        qkv = _matmul(h, [(*w_qkv, 0)], [], [out], _ep_store, tm=tm, tn=tn, tk=d, nj=n // tn)[0]
    else:
        ep = functools.partial(_ep_headnorm, n_norm_blocks=(n - kvw) // tn)
        extras = [(gain, (SUBLANES, tn), lambda i, j: (0, j))]
        qkv = _matmul(h, [(*w_qkv, 0)], extras, [out], ep, tm=tm, tn=tn, tk=d, nj=n // tn)[0]
    o = _gqa_attention(qkv, n_batch, stream.seq, n_q_heads, ctx_kv=ctx_kv, tabs=tabs, sink=sink, banded=banded)
    z = _mm_residual(o, w_o, x, mod, stream, 5, 1.0, alpha)
    return z, qkv


def _mixer_mla(h, x, mod, stream, n_batch, alpha, p, *, ctx, tabs):
    n_heads = p["w_o"][0].shape[-2] // MLA_V
    m, d = h.shape
    lora = p["kv_gain"].shape[1]
    q_lora = p["w_q_a"][0].shape[-1]
    tm = _tile(m, TM, SUBLANES)
    out = (jax.ShapeDtypeStruct((m, q_lora), BF16), (tm, q_lora), lambda i, j: (i, 0))
    qa = _matmul(h, [(*p["w_q_a"], 0)], [(p["q_gain"], (SUBLANES, q_lora), lambda i, j: (0, 0))], [out], _ep_rownorm,
                 tm=tm, tn=q_lora, tk=d, nj=1)[0]
    q = _mm_plain(qa, p["w_q_b"], F32, tm=TM, tn=2 * TN)
    kvw = p["w_kv_a"][0].shape[-1]
    outs = [(jax.ShapeDtypeStruct((m, lora), F32), (tm, lora), lambda i, j: (i, 0)),
            (jax.ShapeDtypeStruct((m, LANES), F32), (tm, LANES), lambda i, j: (i, 0))]
    ckv, krope = _matmul(h, [(*p["w_kv_a"], 0)], [(p["kv_gain"], (SUBLANES, lora), lambda i, j: (0, 0))], outs,
                         functools.partial(_ep_kva, lora=lora), tm=tm, tn=kvw, tk=d, nj=1)
    o = _mla_attention(q, ckv, krope, p["w_kv_b"], n_batch, stream.seq, n_heads, ctx=ctx, tabs=tabs)
    z = _mm_residual(o, p["w_o"], x, mod, stream, 5, 1.0, alpha)
    return z, ckv, krope


def _pad_cols(w, n):
    return jnp.pad(w, [(0, 0)] * (w.ndim - 1) + [(0, n - w.shape[-1])])


def kernel(x_prompt, x_sample, cache_k, cache_v, cache_ckv, cache_krope, c, c_ctx, w_ada, b_ada, ln_g, ln_b, ffn_w_gu, ffn_w_dn, a_w_qkv, a_w_o, a_q_norm, a_k_norm, b_w_qkv, b_w_o, b_sink, c_w_q_a, c_q_a_norm, c_w_q_b, c_w_kv_a, c_kv_norm, c_w_kv_b, c_w_o):
    n_ctx_b, seq_ctx, d = x_prompt.shape
    n_lat_b, seq_lat, _ = x_sample.shape
    depth = w_ada.shape[0]
    d_ff = ffn_w_dn.shape[2]
    f_pad = _round_up(d_ff, TK)
    alpha = (2 * depth) ** 0.25
    n_q_heads = a_w_o.shape[1] // HEAD_DIM
    kvw = N_KV_HEADS * HEAD_DIM
    past = cache_k.shape[2]

    streams = (_Stream(n_ctx_b * seq_ctx, seq_ctx, n_ctx_b * seq_ctx, n_lat_b), _Stream(n_lat_b * seq_lat, seq_lat, seq_lat, 0))
    n_batches = (n_ctx_b, n_lat_b)
    n_mod_rows = _round_up(n_lat_b + 1, 2 * SUBLANES)
    cvec = jnp.zeros((n_mod_rows, d), F32).at[:n_lat_b].set(c).at[n_lat_b].set(c_ctx)
    b_ada3 = b_ada.reshape(depth, 1, N_MOD * d)
    fuse_ada = _ada_tile(N_MOD * d, _swiglu_steps(n_lat_b * seq_lat, d_ff, f_pad)) is not None
    mod_all = {l: _adaln(cvec, w_ada, b_ada3, l).reshape(n_mod_rows, 1, N_MOD * d)
               for l in range(1 if fuse_ada else depth)}

    w_dn_f32 = ffn_w_dn.reshape(depth * 2, d_ff, d)
    fuse_cast = _can_cast_alongside(n_ctx_b * seq_ctx, d_ff, f_pad)
    w_dn = {} if fuse_cast else {None: _cast_pad_rows(w_dn_f32, f_pad)}
    a_qkv, a_o, b_qkv, b_o = (w.astype(BF16) for w in (a_w_qkv, a_w_o, b_w_qkv, b_w_o))
    c_q_a, c_kv_b, c_o = (w.astype(BF16) for w in (c_w_q_a, c_w_kv_b, c_w_o))

    tabs_gqa = _rope_tables(seq_lat, HEAD_DIM)
    tabs_mla = _rope_tables(seq_lat, MLA_ROPE)

    xs = [x_prompt.reshape(-1, d), x_sample.reshape(-1, d)]
    hs = [_modulate(xs[s], mod_all[0], streams[s], 0) for s in range(2)]
    new_k, new_v, new_ckv, new_krope = [], [], [], []

    for i in range(depth):
        kind, idx = i % N_MIXERS, i // N_MIXERS
        mod = mod_all[i]

        def ffn(s, which, next_mod, next_chunk):
            g = 2 * i + which
            if fuse_cast and g not in w_dn:
                act, w_dn[g] = _mm_swiglu(hs[s], (ffn_w_gu, (i, which)), f_pad, cast_rows=(w_dn_f32, g))
            elif fuse_ada and s == 1 and i + 1 not in mod_all and i + 1 < depth:
                act, m_next = _mm_swiglu(hs[s], (ffn_w_gu, (i, which)), f_pad, ada=(cvec, w_ada, b_ada3, i + 1))
                mod_all[i + 1] = m_next.reshape(n_mod_rows, 1, N_MOD * d)
            else:
                act = _mm_swiglu(hs[s], (ffn_w_gu, (i, which)), f_pad)
            w_down = (w_dn[g], ()) if fuse_cast else (w_dn[None], (g,))
            z = _mm_residual(act, w_down, xs[s], mod, streams[s], 2 + 6 * which, 0.5, alpha)
            xs[s], hs[s] = _ln_mod(z, ln_g[i, 2 * which], ln_b[i, 2 * which], next_mod, streams[s], next_chunk)

        for s in range(2):
            ffn(s, 0, mod, 3)

        if kind in (0, 1):
            slot = i - i // N_MIXERS
            if kind == 0:
                w_qkv, w_o = (a_qkv, (idx,)), (a_o, (idx,))
                gain = jnp.concatenate([jnp.tile(a_q_norm[idx], n_q_heads), jnp.tile(a_k_norm[idx], N_KV_HEADS),
                                        jnp.ones((kvw,), F32)])
                gain = jnp.broadcast_to(gain[None], (SUBLANES, gain.shape[0]))
                sink = None
            else:
                w_qkv, w_o = (b_qkv, (idx,)), (b_o, (idx,))
                gain, sink = None, b_sink[idx]
            ctx_kv = (cache_k[:, slot].reshape(n_lat_b, past, kvw), cache_v[:, slot].reshape(n_lat_b, past, kvw))
            for s in range(2):
                lat = s == 1
                z, qkv = _mixer_gqa(hs[s], xs[s], mod, streams[s], n_batches[s], alpha, w_qkv, w_o, gain,
                                    ctx_kv=ctx_kv if lat else None, tabs=tabs_gqa if lat else None,
                                    sink=sink, banded=lat and kind == 1)
                xs[s], hs[s] = _ln_mod(z, ln_g[i, 1], ln_b[i, 1], mod, streams[s], 6)
                if not lat:
                    qw = n_q_heads * HEAD_DIM
                    new_k.append(qkv[:, qw:qw + kvw].reshape(n_ctx_b, seq_ctx, N_KV_HEADS, HEAD_DIM))
                    new_v.append(qkv[:, qw + kvw:].reshape(n_ctx_b, seq_ctx, N_KV_HEADS, HEAD_DIM))
        else:
            n_heads = c_w_o.shape[1] // MLA_V
            lora = c_kv_norm.shape[1]
            w_q_b = c_w_q_b[idx].astype(BF16).reshape(-1, n_heads, MLA_NOPE + MLA_ROPE)
            p = {
                "w_q_a": (c_q_a, (idx,)),
                "q_gain": jnp.broadcast_to(c_q_a_norm[idx][None], (SUBLANES, c_q_a_norm.shape[1])),
                "w_q_b": (jnp.concatenate([w_q_b[:, :, :MLA_NOPE].reshape(-1, n_heads * MLA_NOPE),
                                           w_q_b[:, :, MLA_NOPE:].reshape(-1, n_heads * MLA_ROPE)], axis=1), ()),
                "w_kv_a": (_pad_cols(c_w_kv_a[idx].astype(BF16), lora + LANES), ()),
                "kv_gain": jnp.broadcast_to(c_kv_norm[idx][None], (SUBLANES, lora)),
                "w_kv_b": (c_kv_b, (idx,)),
                "w_o": (c_o, (idx,)),
            }
            ctx = (cache_ckv[:, idx], _pad_cols(cache_krope[:, idx], LANES))
            for s in range(2):
                lat = s == 1
                z, ckv, krope = _mixer_mla(hs[s], xs[s], mod, streams[s], n_batches[s], alpha, p,
                                           ctx=ctx if lat else None, tabs=tabs_mla if lat else None)
                xs[s], hs[s] = _ln_mod(z, ln_g[i, 1], ln_b[i, 1], mod, streams[s], 6)
                if not lat:
                    new_ckv.append(ckv.reshape(n_ctx_b, seq_ctx, lora))
                    new_krope.append(krope[:, :MLA_ROPE].reshape(n_ctx_b, seq_ctx, MLA_ROPE))

        last = i == depth - 1
        for s in range(2):
            ffn(s, 1, None if last else mod_all[i + 1], None if last else 0)

    return (xs[0].reshape(x_prompt.shape), xs[1].reshape(x_sample.shape),
            jnp.stack(new_k, axis=1), jnp.stack(new_v, axis=1), jnp.stack(new_ckv, axis=1), jnp.stack(new_krope, axis=1))
```

```python
import functools

import jax
import jax.numpy as jnp
from jax import lax
from jax.experimental import pallas as pl
from jax.experimental.pallas import tpu as pltpu

F32 = jnp.float32
BF16 = jnp.bfloat16

GRID_W = 64
N_MIXERS = 3
HEAD_DIM = 128
N_KV_HEADS = 8
WINDOW = 128
ROPE_THETA = 10000.0
MLA_NOPE = 128
MLA_ROPE = 64
MLA_V = 128
N_MOD = 9
NEG_INF = -1e30
RMS_EPS = 1e-6
LN_EPS = 1e-5

LANES = 128
SUBLANES = 8
VMEM_PHYSICAL_BYTES = 64 * 1024 * 1024
VMEM_CAP_BYTES = VMEM_PHYSICAL_BYTES - 6 * 1024 * 1024

TM_WIDE = 2048
TM = 1024
TN = 1024
TK = 1024
TN_FF = 256
TN_ADA = 512
TR_LN = 256
TQ_GQA = 256
TQ_MLA = 512
ATTN_ROWS_PER_STEP = 1024
P_SLOTS = 2

LOG2E = 1.4426950408889634


def _round_up(n, m):
    return -(-n // m) * m


def _tile(n, pref, unit=LANES):
    if n <= pref:
        return n
    t = (pref // unit) * unit
    while t > unit and n % t:
        t -= unit
    assert n % t == 0, (n, pref)
    return t


def _nbytes(shape, dtype):
    n = jnp.dtype(dtype).itemsize
    for s in shape:
        if s is not None:
            n *= s
    return n


def _cparams(semantics, block_bytes, temp_bytes=0):
    est = 2 * block_bytes + temp_bytes + 8 * 1024 * 1024
    return pltpu.CompilerParams(dimension_semantics=semantics,
                                vmem_limit_bytes=int(min(max(est, 16 * 1024 * 1024), VMEM_CAP_BYTES)))


def _dot(a, b):
    return jnp.dot(a, b, preferred_element_type=F32)


def _dot_nt(a, b):
    return lax.dot_general(a, b, (((1,), (1,)), ((), ())), preferred_element_type=F32)


def _sigmoid(x):
    return 1.0 / (1.0 + jnp.exp(-x))


class _Stream:
    def __init__(self, n_rows, seq, rows_per_mod, mod_row0):
        self.n_rows, self.seq, self.rows_per_mod, self.mod_row0 = n_rows, seq, rows_per_mod, mod_row0

    def mod_block(self, tm, width, col_block):
        rpm, row0 = self.rows_per_mod, self.mod_row0
        if tm > rpm:
            nb = tm // rpm
            assert tm % rpm == 0 and row0 % nb == 0
            return (nb, 1, width), lambda i, j: (row0 // nb + i, 0, col_block + j)
        assert rpm % tm == 0
        return (1, 1, width), lambda i, j: (row0 + (i * tm) // rpm, 0, col_block + j)


def _adaln_body(c_ref, w_ref, b_ref, o_ref):
    cv = c_ref[...]
    s = (cv * _sigmoid(cv)).astype(BF16)
    o_ref[...] = _dot(s, w_ref[...].astype(BF16)) + b_ref[...]


def _adaln(cvec, w_ada, b_ada3, layer):
    _, d, n = w_ada.shape
    r = cvec.shape[0]
    tn = _tile(n, TN_ADA)
    blocks = _nbytes((r, d), F32) + _nbytes((d, tn), F32) + _nbytes((1, tn), F32) + _nbytes((r, tn), F32)
    return pl.pallas_call(
        _adaln_body,
        grid=(n // tn,),
        in_specs=[pl.BlockSpec((r, d), lambda j: (0, 0)),
                  pl.BlockSpec((None, d, tn), lambda j: (layer, 0, j)),
                  pl.BlockSpec((None, 1, tn), lambda j: (layer, 0, j))],
        out_specs=pl.BlockSpec((r, tn), lambda j: (0, j)),
        out_shape=jax.ShapeDtypeStruct((r, n), F32),
        compiler_params=_cparams(("parallel",), blocks, _nbytes((d, tn), BF16)),
    )(cvec, w_ada, b_ada3)


def _mm_body(*refs, n_w, n_extra, nk, n_valid, epilogue, side):
    x_ref = refs[0]
    w_refs = refs[1:1 + n_w]
    e_refs = refs[1 + n_w:1 + n_w + n_extra]
    o_refs = refs[1 + n_w + n_extra:]

    def parts():
        x = x_ref[...]
        return [_dot(x, w[...].astype(BF16)) for w in w_refs]

    side_calls = []
    for side_fn, n_in, n_out in reversed(side or ()):
        side_calls.append(functools.partial(side_fn, e_refs[len(e_refs) - n_in:], o_refs[len(o_refs) - n_out:]))
        e_refs, o_refs = e_refs[:len(e_refs) - n_in], o_refs[:len(o_refs) - n_out]

    def run_side():
        for call in side_calls:
            call()

    if n_valid is not None:
        j = pl.program_id(1)

        @pl.when(j < n_valid)
        def _():
            epilogue(parts, e_refs, o_refs)
            run_side()

        @pl.when(j >= n_valid)
        def _():
            for o in o_refs:
                o[...] = jnp.zeros(o.shape, o.dtype)
            run_side()
        return

    if nk == 1:
        epilogue(parts, e_refs, o_refs)
        run_side()
        return

    assert not side_calls

    k = pl.program_id(2)
    acc = o_refs[0]

    @pl.when(k == 0)
    def _():
        acc[...] = parts()[0]

    if nk > 2:
        @pl.when(jnp.logical_and(k > 0, k < nk - 1))
        def _():
            acc[...] += parts()[0]

    @pl.when(k == nk - 1)
    def _():
        epilogue(lambda: [acc[...] + parts()[0]], e_refs, o_refs)


def _matmul(x, ws, extras, outs, epilogue, *, tm, tn, tk, nj, n_valid=None, x_single_buffer=False, side=None,
            sequential=False):
    m, kdim = x.shape
    assert m % tm == 0 and kdim % tk == 0, (x.shape, tm, tk)
    nk = kdim // tk
    if nk > 1:
        assert len(ws) == 1 and outs[0][0].dtype == F32 and n_valid is None
    last = nj - 1 if n_valid is None else n_valid - 1

    def ij(im):
        return lambda i, j, k: im(i, j)

    def w_spec(lead, off):
        return pl.BlockSpec((None,) * len(lead) + (tk, tn), lambda i, j, k: (*lead, k, off + jnp.minimum(j, last)))

    x_mode = {"pipeline_mode": pl.Buffered(1)} if x_single_buffer else {}
    in_specs = [pl.BlockSpec((tm, tk), lambda i, j, k: (i, k), **x_mode)]
    in_specs += [w_spec(lead, off) for _, lead, off in ws]
    in_specs += [pl.BlockSpec(blk, ij(im)) for _, blk, im in extras]
    out_specs = [pl.BlockSpec(blk, ij(im)) for _, blk, im in outs]
    streamed = sum(_nbytes((tk, tn), w.dtype) for w, _, _ in ws)
    streamed += sum(_nbytes(blk, a.dtype) for a, blk, _ in extras) + sum(_nbytes(blk, s.dtype) for s, blk, _ in outs)
    x_bytes = _nbytes((tm, tk), x.dtype)
    blocks = streamed + (x_bytes if not x_single_buffer else x_bytes // 2)
    temps = (len(ws) + 1) * _nbytes((tm, tn), F32) + sum(_nbytes((tk, tn), BF16) for w, _, _ in ws if w.dtype != BF16)
    return pl.pallas_call(
        functools.partial(_mm_body, n_w=len(ws), n_extra=len(extras), nk=nk, n_valid=n_valid, epilogue=epilogue, side=side),
        grid=(m // tm, nj, nk),
        in_specs=in_specs,
        out_specs=out_specs,
        out_shape=[s for s, _, _ in outs],
        compiler_params=_cparams(("arbitrary",) * 3 if sequential else ("parallel", "parallel", "arbitrary"), blocks, temps),
    )(x, *[w for w, _, _ in ws], *[a for a, _, _ in extras])


def _ep_store(parts, e_refs, o_refs):
    o_refs[0][...] = parts()[0].astype(o_refs[0].dtype)


def _ep_swiglu(parts, e_refs, o_refs):
    g, u = parts()
    o_refs[0][...] = (g * _sigmoid(g) * u).astype(o_refs[0].dtype)


def _ep_residual(parts, e_refs, o_refs, *, alpha, gscale):
    x_ref, m_ref = e_refs
    y = parts()[0]
    nb = m_ref.shape[0]
    rows = y.shape[0] // nb
    for r in range(nb):
        sl = slice(r * rows, (r + 1) * rows)
        o_refs[0][sl, :] = alpha * x_ref[sl, :] + (gscale * m_ref[r]) * y[sl, :]


def _ep_headnorm(parts, e_refs, o_refs, *, n_norm_blocks):
    g_ref = e_refs[0]
    j = pl.program_id(1)

    @pl.when(j < n_norm_blocks)
    def _():
        y = parts()[0]
        for h in range(y.shape[1] // HEAD_DIM):
            sl = slice(h * HEAD_DIM, (h + 1) * HEAD_DIM)
            yh = y[:, sl]
            r = lax.rsqrt(jnp.mean(yh * yh, axis=-1, keepdims=True) + RMS_EPS)
            o_refs[0][:, sl] = yh * r * g_ref[0:1, sl]

    @pl.when(j >= n_norm_blocks)
    def _():
        o_refs[0][...] = parts()[0]


def _ep_rownorm(parts, e_refs, o_refs):
    y = parts()[0]
    r = lax.rsqrt(jnp.mean(y * y, axis=-1, keepdims=True) + RMS_EPS)
    o_refs[0][...] = (y * r * e_refs[0][0:1, :]).astype(o_refs[0].dtype)


def _ep_kva(parts, e_refs, o_refs, *, lora):
    y = parts()[0]
    c = y[:, :lora]
    r = lax.rsqrt(jnp.mean(c * c, axis=-1, keepdims=True) + RMS_EPS)
    o_refs[0][...] = c * r * e_refs[0][0:1, :]
    o_refs[1][...] = y[:, lora:]


def _mm_plain(x, w, out_dtype, *, tm=TM, tn=TN):
    m, kdim = x.shape
    n = w[0].shape[-1]
    tm, tn = _tile(m, tm, SUBLANES), _tile(n, tn)
    out = (jax.ShapeDtypeStruct((m, n), out_dtype), (tm, tn), lambda i, j: (i, j))
    return _matmul(x, [(*w, 0)], [], [out], _ep_store, tm=tm, tn=tn, tk=kdim, nj=n // tn)[0]


def _mm_swiglu(h, w_gu, f_pad, cast_rows=None, ada=None, ln=None):
    m, d = h.shape
    d_ff = w_gu[0].shape[-1] // 2
    tm, tn = _tile(m, TM_WIDE, SUBLANES), _tile(d_ff, TN_FF)
    assert f_pad % tn == 0
    n_valid, nj = d_ff // tn, f_pad // tn
    outs = [(jax.ShapeDtypeStruct((m, f_pad), BF16), (tm, tn), lambda i, j: (i, j))]
    extras, side = [], []
    if ada is not None:
        cvec, w_ada, b_ada3, layer = ada
        n = w_ada.shape[-1]
        ta = _ada_tile(n, (m // tm) * nj)
        ada_blk = _step_block(nj, n // ta)
        extras += [(cvec, cvec.shape, lambda i, j: (0, 0)),
                   (w_ada, (None, d, ta), lambda i, j: (layer, 0, ada_blk(i, j))),
                   (b_ada3, (None, 1, ta), lambda i, j: (layer, 0, ada_blk(i, j)))]
        outs.append((jax.ShapeDtypeStruct((cvec.shape[0], n), F32), (cvec.shape[0], ta), lambda i, j: (0, ada_blk(i, j))))
        side.append((_side_adaln, 3, 1))
    if cast_rows is not None:
        w_arr, w_idx = cast_rows
        rows, cols = w_arr.shape[-2:]
        rb = f_pad // ((m // tm) * nj)
        assert _can_cast_alongside(m, d_ff, f_pad), (m, d_ff, f_pad)
        n_row_blocks = rows // rb
        extras.append((w_arr, (None, rb, cols), lambda i, j: (w_idx, jnp.minimum(i * nj + j, n_row_blocks - 1), 0)))
        outs.append((jax.ShapeDtypeStruct((f_pad, cols), BF16), (rb, cols), lambda i, j: (i * nj + j, 0)))
        side.append((functools.partial(_side_cast_rows, nj=nj, n_row_blocks=n_row_blocks), 1, 1))
    if ln is not None:
        z, g, b, mod, stream, shift_chunk = ln
        rows, dz = z.shape
        rb = _ln_side_rows(rows, (m // tm) * nj)
        ln_blk = _step_block(nj, rows // rb)
        extras += [(z, (rb, dz), lambda i, j: (ln_blk(i, j), 0)),
                   (g.reshape(1, dz), (1, dz), lambda i, j: (0, 0)), (b.reshape(1, dz), (1, dz), lambda i, j: (0, 0))]
        for chunk in (shift_chunk, shift_chunk + 1):
            mblk, mim = stream.mod_block(rb, dz, chunk)
            extras.append((mod, mblk, lambda i, j, mim=mim: mim(ln_blk(i, j), 0)))
        outs += [(jax.ShapeDtypeStruct((rows, dz), F32), (rb, dz), lambda i, j: (ln_blk(i, j), 0)),
                 (jax.ShapeDtypeStruct((rows, dz), BF16), (rb, dz), lambda i, j: (ln_blk(i, j), 0))]
        side.append((_side_ln, 5, 2))
    res = _matmul(h, [(*w_gu, 0), (*w_gu, n_valid)], extras, outs, _ep_swiglu, tm=tm, tn=tn, tk=d, nj=nj,
                  n_valid=n_valid, x_single_buffer=True, side=side, sequential=ada is not None or ln is not None)
    return res if side else res[0]


def _step_block(nj, n_blocks):
    return lambda i, j: jnp.minimum(i * nj + j, n_blocks - 1)


def _ln_side_rows(rows, steps):
    for rb in range(2 * SUBLANES, rows + 1, 2 * SUBLANES):
        if rows % rb == 0 and rows // rb <= steps:
            return rb
    return None


def _side_ln(e_refs, o_refs):
    _ln_body(*e_refs, *o_refs, with_h=True)


def _ada_tile(n, steps):
    for ta in range(LANES, n + 1, LANES):
        if n % ta == 0 and n // ta <= steps:
            return ta
    return None


def _side_adaln(e_refs, o_refs):
    _adaln_body(*e_refs, *o_refs)


def _swiglu_steps(m, d_ff, f_pad):
    tm, tn = _tile(m, TM_WIDE, SUBLANES), _tile(d_ff, TN_FF)
    return (m // tm) * (f_pad // tn)


def _can_cast_alongside(m, d_ff, f_pad):
    steps = _swiglu_steps(m, d_ff, f_pad)
    rb = f_pad // steps
    return f_pad % steps == 0 and rb % (2 * SUBLANES) == 0 and d_ff % rb == 0


def _side_cast_rows(e_refs, o_refs, *, nj, n_row_blocks):
    (w_ref,), (o_ref,) = e_refs, o_refs
    r = pl.program_id(0) * nj + pl.program_id(1)
    w = w_ref[...]
    o_ref[...] = jnp.where(r < n_row_blocks, w, jnp.zeros_like(w)).astype(o_ref.dtype)


def _cast_pad_body(w_ref, o_ref, *, n_valid):
    r = pl.program_id(1)

    @pl.when(r < n_valid)
    def _():
        o_ref[...] = w_ref[...].astype(o_ref.dtype)

    @pl.when(r >= n_valid)
    def _():
        o_ref[...] = jnp.zeros(o_ref.shape, o_ref.dtype)


def _cast_pad_rows(w, rows_pad):
    g, rows, cols = w.shape
    tr = _tile(rows, TR_LN)
    assert rows_pad % tr == 0
    n_valid = rows // tr
    return pl.pallas_call(
        functools.partial(_cast_pad_body, n_valid=n_valid),
        grid=(g, rows_pad // tr),
        in_specs=[pl.BlockSpec((None, tr, cols), lambda l, r: (l, jnp.minimum(r, n_valid - 1), 0))],
        out_specs=pl.BlockSpec((None, tr, cols), lambda l, r: (l, r, 0)),
        out_shape=jax.ShapeDtypeStruct((g, rows_pad, cols), BF16),
        compiler_params=_cparams(("parallel", "parallel"), _nbytes((tr, cols), F32) + _nbytes((tr, cols), BF16)),
    )(w)


def _mm_residual(a, w, x, mod, stream, chunk, gscale, alpha):
    m, kdim = a.shape
    d = w[0].shape[-1]
    tm, tn, tk = _tile(m, TM_WIDE, SUBLANES), _tile(d, TN), _tile(kdim, TK)
    if tm > stream.rows_per_mod and tm % stream.rows_per_mod:
        tm = stream.rows_per_mod
    mblk, mim = stream.mod_block(tm, tn, chunk * (d // tn))
    extras = [(x, (tm, tn), lambda i, j: (i, j)), (mod, mblk, mim)]
    out = (jax.ShapeDtypeStruct((m, d), F32), (tm, tn), lambda i, j: (i, j))
    ep = functools.partial(_ep_residual, alpha=alpha, gscale=gscale)
    return _matmul(a, [(*w, 0)], extras, [out], ep, tm=tm, tn=tn, tk=tk, nj=d // tn)[0]


def _ln_body(*refs, with_h):
    if with_h:
        z_ref, g_ref, b_ref, sh_ref, sc_ref, x_ref, h_ref = refs
    else:
        z_ref, g_ref, b_ref, x_ref = refs
    z = z_ref[...]
    zc = z - jnp.mean(z, axis=-1, keepdims=True)
    y = zc * lax.rsqrt(jnp.mean(zc * zc, axis=-1, keepdims=True) + LN_EPS)
    y = y * g_ref[...] + b_ref[...]
    x_ref[...] = y
    if with_h:
        h_ref[...] = (y * (1 + sc_ref[0]) + sh_ref[0]).astype(BF16)


def _ln_mod(z, g, b, mod=None, stream=None, shift_chunk=None):
    m, d = z.shape
    tr = _tile(m, TR_LN, SUBLANES)
    with_h = mod is not None
    row = pl.BlockSpec((tr, d), lambda i: (i, 0))
    vec = pl.BlockSpec((1, d), lambda i: (0, 0))
    in_specs, args = [row, vec, vec], [z, g.reshape(1, d), b.reshape(1, d)]
    out_specs, out_shape = [row], [jax.ShapeDtypeStruct((m, d), F32)]
    blocks = 2 * _nbytes((tr, d), F32)
    if with_h:
        for chunk in (shift_chunk, shift_chunk + 1):
            blk, im = stream.mod_block(tr, d, chunk)
            in_specs.append(pl.BlockSpec(blk, lambda i, im=im: im(i, 0)))
            args.append(mod)
        out_specs.append(row)
        out_shape.append(jax.ShapeDtypeStruct((m, d), BF16))
        blocks += _nbytes((tr, d), BF16)
    res = pl.pallas_call(
        functools.partial(_ln_body, with_h=with_h),
        grid=(m // tr,), in_specs=in_specs, out_specs=out_specs, out_shape=out_shape,
        compiler_params=_cparams(("parallel",), blocks, 2 * _nbytes((tr, d), F32)),
    )(*args)
    return (res[0], res[1]) if with_h else (res[0], None)


def _modulate_body(x_ref, sh_ref, sc_ref, h_ref):
    h_ref[...] = (x_ref[...] * (1 + sc_ref[0]) + sh_ref[0]).astype(BF16)


def _modulate(x, mod, stream, shift_chunk):
    m, d = x.shape
    tr = _tile(m, TR_LN, SUBLANES)
    row = pl.BlockSpec((tr, d), lambda i: (i, 0))
    in_specs, args = [row], [x]
    for chunk in (shift_chunk, shift_chunk + 1):
        blk, im = stream.mod_block(tr, d, chunk)
        in_specs.append(pl.BlockSpec(blk, lambda i, im=im: im(i, 0)))
        args.append(mod)
    return pl.pallas_call(
        _modulate_body, grid=(m // tr,), in_specs=in_specs, out_specs=row,
        out_shape=jax.ShapeDtypeStruct((m, d), BF16),
        compiler_params=_cparams(("parallel",), _nbytes((tr, d), F32) + _nbytes((tr, d), BF16)),
    )(*args)


def _rope_tables(n_tok, rot_dim):
    n_rows = n_tok // GRID_W
    row = jnp.repeat(jnp.arange(n_rows, dtype=F32), GRID_W)
    col = jnp.tile(jnp.arange(GRID_W, dtype=F32), n_rows)
    quarter = rot_dim // 4
    inv_freq = ROPE_THETA ** (-jnp.arange(quarter, dtype=F32) / quarter)
    ang_r, ang_c = row[:, None] * inv_freq, col[:, None] * inv_freq
    cos = jnp.concatenate([jnp.cos(ang_r)] * 2 + [jnp.cos(ang_c)] * 2, axis=-1)
    sin = jnp.concatenate([-jnp.sin(ang_r), jnp.sin(ang_r), -jnp.sin(ang_c), jnp.sin(ang_c)], axis=-1)
    reps = LANES // rot_dim
    return jnp.tile(cos, (1, reps)), jnp.tile(sin, (1, reps))


def _rope(x, cos, sin, quarter):
    lane = lax.broadcasted_iota(jnp.int32, x.shape, x.ndim - 1)
    first = (lane & quarter) == 0
    partner = jnp.where(first, pltpu.roll(x, LANES - quarter, axis=x.ndim - 1), pltpu.roll(x, quarter, axis=x.ndim - 1))
    return x * cos + partner * sin


def _softmax_pv(s_loc, v_loc, s_ctx, v_ctx, sink):
    g, tq, _ = s_loc.shape
    m = jnp.max(s_loc, axis=-1, keepdims=True)
    if s_ctx is not None:
        m = jnp.maximum(m, jnp.max(s_ctx, axis=-1, keepdims=True))
    if sink is not None:
        m = jnp.maximum(m, sink)
    p = jnp.exp2(s_loc - m)
    l = jnp.sum(p, axis=-1, keepdims=True)
    o = _dot(p.astype(BF16).reshape(g * tq, -1), v_loc)
    if s_ctx is not None:
        p = jnp.exp2(s_ctx - m)
        l = l + jnp.sum(p, axis=-1, keepdims=True)
        o = o + _dot(p.astype(BF16).reshape(g * tq, -1), v_ctx)
    if sink is not None:
        l = l + jnp.exp2(sink - m)
    return o.reshape(g, tq, -1) / l


def _softmax_pv_rows(s_loc, v_loc, s_ctx, v_ctx, p_loc_ref, p_ctx_ref):
    m = jnp.max(s_loc, axis=-1, keepdims=True)
    if s_ctx is not None:
        m = jnp.maximum(m, jnp.max(s_ctx, axis=-1, keepdims=True))
    p = jnp.exp2(s_loc - m)
    l = jnp.sum(p, axis=-1, keepdims=True)
    p_loc_ref[...] = p.astype(p_loc_ref.dtype)
    o = _dot(p_loc_ref[...], v_loc)
    if s_ctx is not None:
        p = jnp.exp2(s_ctx - m)
        l = l + jnp.sum(p, axis=-1, keepdims=True)
        p_ctx_ref[...] = p.astype(p_ctx_ref.dtype)
        o = o + _dot(p_ctx_ref[...], v_ctx)
    return o / l


def _gqa_body(*refs, seq, n_ctx, groups, hps, tq, rope, banded, has_sink, scale):
    it = iter(refs)
    q_ref, k_ref, v_ref = next(it), next(it), next(it)
    kc_ref, vc_ref = (next(it), next(it)) if n_ctx else (None, None)
    cos_ref, sin_ref = (next(it), next(it)) if rope else (None, None)
    sink_ref = next(it) if has_sink else None
    o_ref = next(it)
    quarter = HEAD_DIM // 4
    qw = groups * HEAD_DIM

    for hh in range(hps):
        hd = slice(hh * HEAD_DIM, (hh + 1) * HEAD_DIM)
        k = k_ref[:, hd]
        if rope:
            k = _rope(k, cos_ref[...], sin_ref[...], quarter)
        kb, vb = k.astype(BF16), v_ref[:, hd].astype(BF16)
        kcb, vcb = (kc_ref[:, hd].astype(BF16), vc_ref[:, hd].astype(BF16)) if n_ctx else (None, None)
        sink = sink_ref[hh, :, 0:1, 0:1] if has_sink else None

        for qb in range(seq // tq):
            r0 = qb * tq
            q_raw, q_rot = [], []
            for g in range(groups):
                q = q_ref[r0:r0 + tq, hh * qw + g * HEAD_DIM:hh * qw + (g + 1) * HEAD_DIM] * scale
                q_raw.append(q.astype(BF16))
                if rope:
                    q_rot.append(_rope(q, cos_ref[r0:r0 + tq, :], sin_ref[r0:r0 + tq, :], quarter).astype(BF16))
            q_raw = jnp.concatenate(q_raw, axis=0)
            q_loc = jnp.concatenate(q_rot, axis=0) if rope else q_raw
            k0, k1 = (max(0, r0 - WINDOW), min(seq, r0 + tq + WINDOW)) if banded else (0, seq)
            s_loc = _dot_nt(q_loc, kb[k0:k1]).reshape(groups, tq, k1 - k0)
            if banded:
                q_pos = r0 + lax.broadcasted_iota(jnp.int32, (tq, k1 - k0), 0)
                k_pos = k0 + lax.broadcasted_iota(jnp.int32, (tq, k1 - k0), 1)
                s_loc = jnp.where((jnp.abs(q_pos - k_pos) <= WINDOW)[None], s_loc, NEG_INF)
            s_ctx = _dot_nt(q_raw, kcb).reshape(groups, tq, n_ctx) if n_ctx else None
            o = _softmax_pv(s_loc, vb[k0:k1], s_ctx, vcb, sink)
            for g in range(groups):
                o_ref[r0:r0 + tq, hh * qw + g * HEAD_DIM:hh * qw + (g + 1) * HEAD_DIM] = o[g].astype(o_ref.dtype)


def _gqa_attention(qkv, n_batch, seq, n_q_heads, *, ctx_kv=None, tabs=None, sink=None, banded=False):
    groups = n_q_heads // N_KV_HEADS
    tq = _tile(seq, TQ_GQA, SUBLANES)
    hps = max(1, min(N_KV_HEADS, ATTN_ROWS_PER_STEP // seq))
    assert N_KV_HEADS % hps == 0
    qw, kw = hps * groups * HEAD_DIM, hps * HEAD_DIM
    k_blk0, v_blk0 = n_q_heads // hps, (n_q_heads + N_KV_HEADS) // hps
    in_specs = [pl.BlockSpec((seq, qw), lambda b, j: (b, j)),
                pl.BlockSpec((seq, kw), lambda b, j: (b, k_blk0 + j)),
                pl.BlockSpec((seq, kw), lambda b, j: (b, v_blk0 + j))]
    args = [qkv, qkv, qkv]
    blocks = _nbytes((seq, qw + 2 * kw), F32) + _nbytes((seq, qw), BF16)
    n_ctx = 0
    if ctx_kv is not None:
        n_ctx = ctx_kv[0].shape[1]
        in_specs += [pl.BlockSpec((None, n_ctx, kw), lambda b, j: (b, 0, j))] * 2
        args += list(ctx_kv)
        blocks += 2 * _nbytes((n_ctx, kw), F32)
    if tabs is not None:
        in_specs += [pl.BlockSpec((seq, LANES), lambda b, j: (0, 0))] * 2
        args += list(tabs)
        blocks += 2 * _nbytes((seq, LANES), F32)
    if sink is not None:
        sink_b = jnp.broadcast_to((sink.astype(F32) * LOG2E).reshape(N_KV_HEADS, groups, 1, 1),
                                  (N_KV_HEADS, groups, SUBLANES, LANES))
        in_specs.append(pl.BlockSpec((hps, groups, SUBLANES, LANES), lambda b, j: (j, 0, 0, 0)))
        args.append(sink_b)
        blocks += _nbytes((hps, groups, SUBLANES, LANES), F32)
    body = functools.partial(_gqa_body, seq=seq, n_ctx=n_ctx, groups=groups, hps=hps, tq=tq, rope=tabs is not None,
                             banded=banded, has_sink=sink is not None, scale=HEAD_DIM ** -0.5 * LOG2E)
    temps = 3 * _nbytes((groups * tq, seq + n_ctx), F32)
    return pl.pallas_call(
        body, grid=(n_batch, N_KV_HEADS // hps), in_specs=in_specs,
        out_specs=pl.BlockSpec((seq, qw), lambda b, j: (b, j)),
        out_shape=jax.ShapeDtypeStruct((n_batch * seq, n_q_heads * HEAD_DIM), BF16),
        compiler_params=_cparams(("parallel", "parallel"), blocks, temps),
    )(*args)


def _mla_body(*refs, seq, n_ctx, pps, tq, rope, scale):
    it = iter(refs)
    qn_ref, qr_ref, c_ref, kr_ref, w_ref = next(it), next(it), next(it), next(it), next(it)
    cc_ref, krc_ref = (next(it), next(it)) if n_ctx else (None, None)
    cos_ref, sin_ref = (next(it), next(it)) if rope else (None, None)
    o_ref = next(it)
    p_loc_ref = next(it)
    p_ctx_ref = next(it) if n_ctx else None
    quarter = MLA_ROPE // 4
    hw = MLA_NOPE + MLA_V

    def rope_keys(kr):
        return kr.astype(BF16), pltpu.roll(kr, MLA_ROPE, axis=1).astype(BF16)

    w = w_ref[...]
    kv = _dot(c_ref[...].astype(BF16), w)
    kr = kr_ref[...]
    if rope:
        kr = _rope(kr, cos_ref[...], sin_ref[...], quarter)
    kr_pair = rope_keys(kr)
    if n_ctx:
        kvc = _dot(cc_ref[...].astype(BF16), w)
        krc_pair = rope_keys(krc_ref[...])

    n_qb = seq // tq
    for hd in range(2 * pps):
        pp, h = divmod(hd, 2)
        k_cat = jnp.concatenate([kv[:, hd * hw:hd * hw + MLA_NOPE].astype(BF16), kr_pair[h]], axis=1)
        vv = kv[:, hd * hw + MLA_NOPE:(hd + 1) * hw].astype(BF16)
        kc_cat = vc = None
        if n_ctx:
            kc_cat = jnp.concatenate([kvc[:, hd * hw:hd * hw + MLA_NOPE].astype(BF16), krc_pair[h]], axis=1)
            vc = kvc[:, hd * hw + MLA_NOPE:(hd + 1) * hw].astype(BF16)
        for qb in range(n_qb):
            rows = slice(qb * tq, (qb + 1) * tq)
            slot = (hd * n_qb + qb) % P_SLOTS
            qn = (qn_ref[rows, hd * MLA_NOPE:(hd + 1) * MLA_NOPE] * scale).astype(BF16)
            qr = qr_ref[rows, pp * LANES:(pp + 1) * LANES] * scale
            qr_raw = qr.astype(BF16)
            qr_loc = _rope(qr, cos_ref[rows, :], sin_ref[rows, :], quarter).astype(BF16) if rope else qr_raw
            s_loc = _dot_nt(jnp.concatenate([qn, qr_loc], axis=1), k_cat)
            s_ctx = _dot_nt(jnp.concatenate([qn, qr_raw], axis=1), kc_cat) if n_ctx else None
            o = _softmax_pv_rows(s_loc, vv, s_ctx, vc, p_loc_ref.at[slot], p_ctx_ref.at[slot] if n_ctx else None)
            o_ref[rows, hd * MLA_V:(hd + 1) * MLA_V] = o.astype(o_ref.dtype)


def _mla_attention(q, ckv, krope, w_kv_b, n_batch, seq, n_heads, *, ctx=None, tabs=None):
    lora = ckv.shape[1]
    tq = _tile(seq, TQ_MLA, SUBLANES)
    n_pairs = n_heads // 2
    pps = max(1, min(n_pairs, ATTN_ROWS_PER_STEP // seq))
    assert n_pairs % pps == 0
    w_arr, w_lead = w_kv_b
    pw = 2 * (MLA_NOPE + MLA_V)
    in_specs = [pl.BlockSpec((seq, pps * 2 * MLA_NOPE), lambda b, p: (b, p)),
                pl.BlockSpec((seq, pps * LANES), lambda b, p: (b, n_pairs * 2 * MLA_NOPE // (pps * LANES) + p)),
                pl.BlockSpec((seq, lora), lambda b, p: (b, 0)),
                pl.BlockSpec((seq, LANES), lambda b, p: (b, 0)),
                pl.BlockSpec((None,) * len(w_lead) + (lora, pps * pw), lambda b, p: (*w_lead, 0, p))]
    args = [q, q, ckv, krope, w_arr]
    blocks = _nbytes((seq, pps * (2 * MLA_NOPE + LANES) + LANES + lora), F32) + _nbytes((lora, pps * pw), BF16)
    blocks += _nbytes((seq, pps * 2 * MLA_V), BF16)
    n_ctx = 0
    if ctx is not None:
        n_ctx = ctx[0].shape[1]
        in_specs += [pl.BlockSpec((None, n_ctx, lora), lambda b, p: (b, 0, 0)),
                     pl.BlockSpec((None, n_ctx, LANES), lambda b, p: (b, 0, 0))]
        args += list(ctx)
        blocks += _nbytes((n_ctx, lora + LANES), F32)
    if tabs is not None:
        in_specs += [pl.BlockSpec((seq, LANES), lambda b, p: (0, 0))] * 2
        args += list(tabs)
        blocks += 2 * _nbytes((seq, LANES), F32)
    body = functools.partial(_mla_body, seq=seq, n_ctx=n_ctx, pps=pps, tq=tq, rope=tabs is not None,
                             scale=(MLA_NOPE + MLA_ROPE) ** -0.5 * LOG2E)
    scratch = [pltpu.VMEM((P_SLOTS, tq, seq), BF16)]
    if n_ctx:
        scratch.append(pltpu.VMEM((P_SLOTS, tq, n_ctx), BF16))
    temps = 3 * _nbytes((tq, seq + n_ctx), F32) + 2 * _nbytes((seq + n_ctx, pps * pw), F32)
    temps += _nbytes((P_SLOTS, tq, seq + n_ctx), BF16)
    return pl.pallas_call(
        body, grid=(n_batch, n_pairs // pps), in_specs=in_specs, scratch_shapes=scratch,
        out_specs=pl.BlockSpec((seq, pps * 2 * MLA_V), lambda b, p: (b, p)),
        out_shape=jax.ShapeDtypeStruct((n_batch * seq, n_heads * MLA_V), BF16),
        compiler_params=_cparams(("parallel", "parallel"), blocks, temps),
    )(*args)


def _mixer_gqa(h, x, mod, stream, n_batch, alpha, w_qkv, w_o, gain, *, ctx_kv, tabs, sink, banded):
    n_q_heads = w_o[0].shape[-2] // HEAD_DIM
    kvw = N_KV_HEADS * HEAD_DIM
    m, d = h.shape
    n = w_qkv[0].shape[-1]
    tm, tn = _tile(m, TM, SUBLANES), kvw
    out = (jax.ShapeDtypeStruct((m, n), F32), (tm, tn), lambda i, j: (i, j))
    if gain is None:
        qkv = _matmul(h, [(*w_qkv, 0)], [], [out], _ep_store, tm=tm, tn=tn, tk=d, nj=n // tn)[0]
    else:
        ep = functools.partial(_ep_headnorm, n_norm_blocks=(n - kvw) // tn)
        extras = [(gain, (SUBLANES, tn), lambda i, j: (0, j))]
        qkv = _matmul(h, [(*w_qkv, 0)], extras, [out], ep, tm=tm, tn=tn, tk=d, nj=n // tn)[0]
    o = _gqa_attention(qkv, n_batch, stream.seq, n_q_heads, ctx_kv=ctx_kv, tabs=tabs, sink=sink, banded=banded)
    z = _mm_residual(o, w_o, x, mod, stream, 5, 1.0, alpha)
    return z, qkv


def _mixer_mla(h, x, mod, stream, n_batch, alpha, p, *, ctx, tabs):
    n_heads = p["w_o"][0].shape[-2] // MLA_V
    m, d = h.shape
    lora = p["kv_gain"].shape[1]
    q_lora = p["w_q_a"][0].shape[-1]
    tm = _tile(m, TM, SUBLANES)
    out = (jax.ShapeDtypeStruct((m, q_lora), BF16), (tm, q_lora), lambda i, j: (i, 0))
    qa = _matmul(h, [(*p["w_q_a"], 0)], [(p["q_gain"], (SUBLANES, q_lora), lambda i, j: (0, 0))], [out], _ep_rownorm,
                 tm=tm, tn=q_lora, tk=d, nj=1)[0]
    q = _mm_plain(qa, p["w_q_b"], F32, tm=TM, tn=2 * TN)
    kvw = p["w_kv_a"][0].shape[-1]
    outs = [(jax.ShapeDtypeStruct((m, lora), F32), (tm, lora), lambda i, j: (i, 0)),
            (jax.ShapeDtypeStruct((m, LANES), F32), (tm, LANES), lambda i, j: (i, 0))]
    ckv, krope = _matmul(h, [(*p["w_kv_a"], 0)], [(p["kv_gain"], (SUBLANES, lora), lambda i, j: (0, 0))], outs,
                         functools.partial(_ep_kva, lora=lora), tm=tm, tn=kvw, tk=d, nj=1)
    o = _mla_attention(q, ckv, krope, p["w_kv_b"], n_batch, stream.seq, n_heads, ctx=ctx, tabs=tabs)
    z = _mm_residual(o, p["w_o"], x, mod, stream, 5, 1.0, alpha)
    return z, ckv, krope


def _pad_cols(w, n):
    return jnp.pad(w, [(0, 0)] * (w.ndim - 1) + [(0, n - w.shape[-1])])


def kernel(x_prompt, x_sample, cache_k, cache_v, cache_ckv, cache_krope, c, c_ctx, w_ada, b_ada, ln_g, ln_b, ffn_w_gu, ffn_w_dn, a_w_qkv, a_w_o, a_q_norm, a_k_norm, b_w_qkv, b_w_o, b_sink, c_w_q_a, c_q_a_norm, c_w_q_b, c_w_kv_a, c_kv_norm, c_w_kv_b, c_w_o):
    n_ctx_b, seq_ctx, d = x_prompt.shape
    n_lat_b, seq_lat, _ = x_sample.shape
    depth = w_ada.shape[0]
    d_ff = ffn_w_dn.shape[2]
    f_pad = _round_up(d_ff, TK)
    alpha = (2 * depth) ** 0.25
    n_q_heads = a_w_o.shape[1] // HEAD_DIM
    kvw = N_KV_HEADS * HEAD_DIM
    past = cache_k.shape[2]

    streams = (_Stream(n_ctx_b * seq_ctx, seq_ctx, n_ctx_b * seq_ctx, n_lat_b), _Stream(n_lat_b * seq_lat, seq_lat, seq_lat, 0))
    n_batches = (n_ctx_b, n_lat_b)
    n_mod_rows = _round_up(n_lat_b + 1, 2 * SUBLANES)
    cvec = jnp.zeros((n_mod_rows, d), F32).at[:n_lat_b].set(c).at[n_lat_b].set(c_ctx)
    b_ada3 = b_ada.reshape(depth, 1, N_MOD * d)
    fuse_ada = _ada_tile(N_MOD * d, _swiglu_steps(n_lat_b * seq_lat, d_ff, f_pad)) is not None
    mod_all = {l: _adaln(cvec, w_ada, b_ada3, l).reshape(n_mod_rows, 1, N_MOD * d)
               for l in range(1 if fuse_ada else depth)}

    w_dn_f32 = ffn_w_dn.reshape(depth * 2, d_ff, d)
    fuse_cast = _can_cast_alongside(n_ctx_b * seq_ctx, d_ff, f_pad)
    w_dn = {} if fuse_cast else {None: _cast_pad_rows(w_dn_f32, f_pad)}
    a_qkv, a_o, b_qkv, b_o = (w.astype(BF16) for w in (a_w_qkv, a_w_o, b_w_qkv, b_w_o))
    c_q_a, c_kv_b, c_o = (w.astype(BF16) for w in (c_w_q_a, c_w_kv_b, c_w_o))

    tabs_gqa = _rope_tables(seq_lat, HEAD_DIM)
    tabs_mla = _rope_tables(seq_lat, MLA_ROPE)

    xs = [x_prompt.reshape(-1, d), x_sample.reshape(-1, d)]
    hs = [_modulate(xs[s], mod_all[0], streams[s], 0) for s in range(2)]
    new_k, new_v, new_ckv, new_krope = [], [], [], []
    fuse_ln = _ln_side_rows(n_ctx_b * seq_ctx, _swiglu_steps(n_lat_b * seq_lat, d_ff, f_pad)) is not None
    pending_ln = []

    for i in range(depth):
        kind, idx = i % N_MIXERS, i // N_MIXERS
        mod = mod_all[i]

        def ffn(s, which, next_mod, next_chunk):
            g = 2 * i + which
            jobs = {}
            if fuse_cast and g not in w_dn:
                jobs["cast_rows"] = (w_dn_f32, g)
            elif fuse_ada and s == 1 and i + 1 not in mod_all and i + 1 < depth:
                jobs["ada"] = (cvec, w_ada, b_ada3, i + 1)
            if s == 1 and pending_ln:
                jobs["ln"] = pending_ln.pop()
            res = _mm_swiglu(hs[s], (ffn_w_gu, (i, which)), f_pad, **jobs)
            res = list(res) if jobs else [res]
            act = res.pop(0)
            if "ada" in jobs:
                mod_all[i + 1] = res.pop(0).reshape(n_mod_rows, 1, N_MOD * d)
            if "cast_rows" in jobs:
                w_dn[g] = res.pop(0)
            if "ln" in jobs:
                xs[0], hs[0] = res
            w_down = (w_dn[g], ()) if fuse_cast else (w_dn[None], (g,))
            z = _mm_residual(act, w_down, xs[s], mod, streams[s], 2 + 6 * which, 0.5, alpha)
            ln_args = (z, ln_g[i, 2 * which], ln_b[i, 2 * which], next_mod, streams[s], next_chunk)
            if s == 0 and fuse_ln and next_mod is not None:
                pending_ln.append(ln_args)
            else:
                xs[s], hs[s] = _ln_mod(*ln_args)

        for s in range(2):
            ffn(s, 0, mod, 3)

        if kind in (0, 1):
            slot = i - i // N_MIXERS
            if kind == 0:
                w_qkv, w_o = (a_qkv, (idx,)), (a_o, (idx,))
                gain = jnp.concatenate([jnp.tile(a_q_norm[idx], n_q_heads), jnp.tile(a_k_norm[idx], N_KV_HEADS),
                                        jnp.ones((kvw,), F32)])
                gain = jnp.broadcast_to(gain[None], (SUBLANES, gain.shape[0]))
                sink = None
            else:
                w_qkv, w_o = (b_qkv, (idx,)), (b_o, (idx,))
                gain, sink = None, b_sink[idx]
            ctx_kv = (cache_k[:, slot].reshape(n_lat_b, past, kvw), cache_v[:, slot].reshape(n_lat_b, past, kvw))
            for s in range(2):
                lat = s == 1
                z, qkv = _mixer_gqa(hs[s], xs[s], mod, streams[s], n_batches[s], alpha, w_qkv, w_o, gain,
                                    ctx_kv=ctx_kv if lat else None, tabs=tabs_gqa if lat else None,
                                    sink=sink, banded=lat and kind == 1)
                xs[s], hs[s] = _ln_mod(z, ln_g[i, 1], ln_b[i, 1], mod, streams[s], 6)
                if not lat:
                    qw = n_q_heads * HEAD_DIM
                    new_k.append(qkv[:, qw:qw + kvw].reshape(n_ctx_b, seq_ctx, N_KV_HEADS, HEAD_DIM))
                    new_v.append(qkv[:, qw + kvw:].reshape(n_ctx_b, seq_ctx, N_KV_HEADS, HEAD_DIM))
        else:
            n_heads = c_w_o.shape[1] // MLA_V
            lora = c_kv_norm.shape[1]
            w_q_b = c_w_q_b[idx].astype(BF16).reshape(-1, n_heads, MLA_NOPE + MLA_ROPE)
            p = {
                "w_q_a": (c_q_a, (idx,)),
                "q_gain": jnp.broadcast_to(c_q_a_norm[idx][None], (SUBLANES, c_q_a_norm.shape[1])),
                "w_q_b": (jnp.concatenate([w_q_b[:, :, :MLA_NOPE].reshape(-1, n_heads * MLA_NOPE),
                                           w_q_b[:, :, MLA_NOPE:].reshape(-1, n_heads * MLA_ROPE)], axis=1), ()),
                "w_kv_a": (_pad_cols(c_w_kv_a[idx].astype(BF16), lora + LANES), ()),
                "kv_gain": jnp.broadcast_to(c_kv_norm[idx][None], (SUBLANES, lora)),
                "w_kv_b": (c_kv_b, (idx,)),
                "w_o": (c_o, (idx,)),
            }
            ctx = (cache_ckv[:, idx], _pad_cols(cache_krope[:, idx], LANES))
            for s in range(2):
                lat = s == 1
                z, ckv, krope = _mixer_mla(hs[s], xs[s], mod, streams[s], n_batches[s], alpha, p,
                                           ctx=ctx if lat else None, tabs=tabs_mla if lat else None)
                xs[s], hs[s] = _ln_mod(z, ln_g[i, 1], ln_b[i, 1], mod, streams[s], 6)
                if not lat:
                    new_ckv.append(ckv.reshape(n_ctx_b, seq_ctx, lora))
                    new_krope.append(krope[:, :MLA_ROPE].reshape(n_ctx_b, seq_ctx, MLA_ROPE))

        last = i == depth - 1
        for s in range(2):
            ffn(s, 1, None if last else mod_all[i + 1], None if last else 0)

    return (xs[0].reshape(x_prompt.shape), xs[1].reshape(x_sample.shape),
            jnp.stack(new_k, axis=1), jnp.stack(new_v, axis=1), jnp.stack(new_ckv, axis=1), jnp.stack(new_krope, axis=1))
```
